```python
import math
import jax, jax.numpy as jnp
from jax import lax
import numpy as np

D_MODEL = 1024
BATCH = 4
SEQ = 4096
DEPTH = 4
DEC_BATCH = 128
DEC_SEQ = 8
PAST_LEN = 2048
PAGE_SIZE = 128

N_MIXERS = 2
N_POOL_LAYERS = (DEPTH + 1) // 2
N_ATTN_LAYERS = DEPTH // 2
POOL_WINDOWS = (2, 4, 8, 16)
N_POOL_GROUPS = len(POOL_WINDOWS)
POOL_GROUP = D_MODEL // N_POOL_GROUPS
POOL_BUF = max(POOL_WINDOWS) - 1
HEAD_DIM = 64
N_HEADS = D_MODEL // (2 * HEAD_DIM)
V_DIM = 2 * HEAD_DIM
D_FF = 4 * D_MODEL
ROPE_THETA = 10000.0
EPS = 1e-6
SUBLN_EPS = 1e-5
Q_BLOCK = 128

kernel_name = "hybrid_pool_diffattn_decoder_step"


def rmsnorm(x, g, eps=EPS):
    xf = x.astype(jnp.float32)
    y = xf * lax.rsqrt(jnp.mean(xf * xf, axis=-1, keepdims=True) + eps)
    return (y * g.astype(jnp.float32)).astype(x.dtype)


def rope(x, pos):
    half = HEAD_DIM // 2
    inv = ROPE_THETA ** (-jnp.arange(half, dtype=jnp.float32) / half)
    ang = pos.astype(jnp.float32)[:, None] * inv[None, :]
    cos = jnp.cos(ang)[None, :, None, :]
    sin = jnp.sin(ang)[None, :, None, :]
    xf = x.astype(jnp.float32)
    x1, x2 = xf[..., :half], xf[..., half:]
    return jnp.concatenate([x1 * cos - x2 * sin, x2 * cos + x1 * sin], axis=-1).astype(x.dtype)


def pool_mix(h_ext, pos, w_pool, scale):
    B, L, D = h_ext.shape
    T = L - POOL_BUF
    hf = h_ext.astype(jnp.float32)
    cs = jnp.concatenate([jnp.zeros((B, 1, D), jnp.float32), jnp.cumsum(hf, axis=1)], axis=1)
    end = cs[:, POOL_BUF + 1:L + 1]
    outs = []
    for g, w in enumerate(POOL_WINDOWS):
        lo, hi = g * POOL_GROUP, (g + 1) * POOL_GROUP
        s = end[..., lo:hi] - cs[:, POOL_BUF + 1 - w:L + 1 - w, lo:hi]
        cnt = jnp.minimum(pos + 1, w).astype(jnp.float32)[None, :, None]
        outs.append(s / cnt)
    pooled = jnp.concatenate(outs, axis=-1)
    d = (pooled - hf[:, POOL_BUF:]).astype(h_ext.dtype).reshape(B, T, N_POOL_GROUPS, POOL_GROUP)
    y = jnp.einsum('btgc,gcd->btgd', d, w_pool).reshape(B, T, D)
    return y * scale


def project_qkv(h, pos, w_qkv, q_norm, k_norm):
    B, T, _ = h.shape
    q, k, v = jnp.split(h @ w_qkv, 3, axis=-1)
    q = rope(rmsnorm(q.reshape(B, T, 2 * N_HEADS, HEAD_DIM), q_norm), pos)
    k = rope(rmsnorm(k.reshape(B, T, 2 * N_HEADS, HEAD_DIM), k_norm), pos)
    v = v.reshape(B, T, N_HEADS, V_DIM)
    return q, k, v


def diff_attn_core(q, k, v, mask, lam):
    s = jnp.einsum('bqhcd,bkhcd->bhcqk', q, k).astype(jnp.float32) * (HEAD_DIM ** -0.5)
    s = jnp.where(mask[None, None, None], s, -jnp.inf)
    p = jax.nn.softmax(s, axis=-1)
    a = p[:, :, 0] - lam * p[:, :, 1]
    return jnp.einsum('bhqk,bkhe->bqhe', a.astype(v.dtype), v)


def attn_out(o, subln_g, lam_init, w_o):
    B, T = o.shape[:2]
    o = rmsnorm(o, subln_g, SUBLN_EPS) * (1.0 - lam_init)
    return o.reshape(B, T, N_HEADS * V_DIM) @ w_o


def ffn(h, w_up, w_down):
    return jnp.square(jax.nn.relu(h @ w_up)) @ w_down


def setup_inputs(seed: int = 0) -> dict:
    key = jax.random.key(seed)
    ks = jax.random.split(key, 24)
    f32 = jnp.float32
    n_pages = PAST_LEN // PAGE_SIZE
    n_used = DEC_BATCH * n_pages
    n_phys = n_used + max(1, n_used // 4)
    x_prompt = jax.random.normal(ks[0], (BATCH, SEQ, D_MODEL), f32)
    x_sample = jax.random.normal(ks[1], (DEC_BATCH, DEC_SEQ, D_MODEL), f32)
    cache_k = jax.random.normal(ks[2], (N_ATTN_LAYERS, n_phys, PAGE_SIZE, 2 * N_HEADS, HEAD_DIM), f32)
    cache_v = jax.random.normal(ks[3], (N_ATTN_LAYERS, n_phys, PAGE_SIZE, N_HEADS, V_DIM), f32)
    page_table = jax.random.permutation(ks[4], n_phys)[:n_used].reshape(DEC_BATCH, n_pages).astype(jnp.int32)
    state_pool = jax.random.normal(ks[5], (N_POOL_LAYERS, DEC_BATCH, POOL_BUF, D_MODEL), f32)
    norm_mix = 1.0 + 0.02 * jax.random.normal(ks[6], (DEPTH, D_MODEL), f32)
    norm_ffn = 1.0 + 0.02 * jax.random.normal(ks[7], (DEPTH, D_MODEL), f32)
    w_pool = jax.random.normal(ks[8], (N_POOL_LAYERS, N_POOL_GROUPS, POOL_GROUP, POOL_GROUP), f32) * POOL_GROUP ** -0.5
    pool_scale = 1.0 + 0.1 * jax.random.normal(ks[9], (N_POOL_LAYERS, D_MODEL), f32)
    w_qkv = jax.random.normal(ks[10], (N_ATTN_LAYERS, D_MODEL, 3 * D_MODEL), f32) * D_MODEL ** -0.5
    q_norm = 1.0 + 0.02 * jax.random.normal(ks[11], (N_ATTN_LAYERS, HEAD_DIM), f32)
    k_norm = 1.0 + 0.02 * jax.random.normal(ks[12], (N_ATTN_LAYERS, HEAD_DIM), f32)
    lambda_q1 = 0.1 * jax.random.normal(ks[13], (N_ATTN_LAYERS, HEAD_DIM), f32)
    lambda_k1 = 0.1 * jax.random.normal(ks[14], (N_ATTN_LAYERS, HEAD_DIM), f32)
    lambda_q2 = 0.1 * jax.random.normal(ks[15], (N_ATTN_LAYERS, HEAD_DIM), f32)
    lambda_k2 = 0.1 * jax.random.normal(ks[16], (N_ATTN_LAYERS, HEAD_DIM), f32)
    subln = 1.0 + 0.02 * jax.random.normal(ks[17], (N_ATTN_LAYERS, V_DIM), f32)
    w_o = jax.random.normal(ks[18], (N_ATTN_LAYERS, D_MODEL, D_MODEL), f32) * D_MODEL ** -0.5
    w_up = jax.random.normal(ks[19], (DEPTH, D_MODEL, D_FF), f32) * D_MODEL ** -0.5
    w_down = jax.random.normal(ks[20], (DEPTH, D_FF, D_MODEL), f32) * D_FF ** -0.5
    return {"x_prompt": x_prompt, "x_sample": x_sample, "cache_k": cache_k, "cache_v": cache_v,
            "page_table": page_table, "state_pool": state_pool, "norm_mix": norm_mix, "norm_ffn": norm_ffn,
            "w_pool": w_pool, "pool_scale": pool_scale, "w_qkv": w_qkv, "q_norm": q_norm, "k_norm": k_norm,
            "lambda_q1": lambda_q1, "lambda_k1": lambda_k1, "lambda_q2": lambda_q2, "lambda_k2": lambda_k2,
            "subln": subln, "w_o": w_o, "w_up": w_up, "w_down": w_down}


def reference(x_prompt, x_sample, cache_k, cache_v, page_table, state_pool, norm_mix, norm_ffn,
              w_pool, pool_scale, w_qkv, q_norm, k_norm, lambda_q1, lambda_k1, lambda_q2, lambda_k2,
              subln, w_o, w_up, w_down):
    Bp, Sp, D = x_prompt.shape
    Bs, Ts, _ = x_sample.shape
    past_len = page_table.shape[1] * PAGE_SIZE
    pos_p = jnp.arange(Sp)
    pos_s = past_len + jnp.arange(Ts)
    n_blk = Sp // Q_BLOCK
    xp, xs = x_prompt, x_sample
    nk_p, nv_p, nk_s, nv_s, np_p, np_s = [], [], [], [], [], []
    for i in range(DEPTH):
        j = i // N_MIXERS
        hp = rmsnorm(xp, norm_mix[i])
        hs = rmsnorm(xs, norm_mix[i])
        if i % N_MIXERS == 0:
            ext_p = jnp.concatenate([jnp.zeros((Bp, POOL_BUF, D), hp.dtype), hp], axis=1)
            ext_s = jnp.concatenate([state_pool[j].astype(hs.dtype), hs], axis=1)
            xp = xp + pool_mix(ext_p, pos_p, w_pool[j], pool_scale[j])
            xs = xs + pool_mix(ext_s, pos_s, w_pool[j], pool_scale[j])
            np_p.append(ext_p[:, -POOL_BUF:])
            np_s.append(ext_s[:, -POOL_BUF:])
        else:
            lam_init = 0.8 - 0.6 * math.exp(-0.3 * i)
            lam = (jnp.exp(jnp.sum(lambda_q1[j].astype(jnp.float32) * lambda_k1[j].astype(jnp.float32)))
                   - jnp.exp(jnp.sum(lambda_q2[j].astype(jnp.float32) * lambda_k2[j].astype(jnp.float32)))
                   + lam_init)
            qp, kp, vp = project_qkv(hp, pos_p, w_qkv[j], q_norm[j], k_norm[j])
            k5 = kp.reshape(Bp, Sp, N_HEADS, 2, HEAD_DIM)
            qb = jnp.moveaxis(qp.reshape(Bp, n_blk, Q_BLOCK, N_HEADS, 2, HEAD_DIM), 1, 0)
            kpos = jnp.arange(Sp)

            def blk(args, k5=k5, vp=vp, lam=lam):
                qi, bi = args
                qpos = bi * Q_BLOCK + jnp.arange(Q_BLOCK)
                return diff_attn_core(qi, k5, vp, qpos[:, None] >= kpos[None, :], lam)

            op = lax.map(blk, (qb, jnp.arange(n_blk)))
            op = jnp.moveaxis(op, 0, 1).reshape(Bp, Sp, N_HEADS, V_DIM)
            xp = xp + attn_out(op, subln[j], lam_init, w_o[j])
            qs, ks_, vs = project_qkv(hs, pos_s, w_qkv[j], q_norm[j], k_norm[j])
            k_past = cache_k[j, page_table].reshape(Bs, past_len, 2 * N_HEADS, HEAD_DIM).astype(ks_.dtype)
            v_past = cache_v[j, page_table].reshape(Bs, past_len, N_HEADS, V_DIM).astype(vs.dtype)
            k_all = jnp.concatenate([k_past, ks_], axis=1).reshape(Bs, past_len + Ts, N_HEADS, 2, HEAD_DIM)
            v_all = jnp.concatenate([v_past, vs], axis=1)
            mask_s = jnp.arange(past_len + Ts)[None, :] <= pos_s[:, None]
            os_ = diff_attn_core(qs.reshape(Bs, Ts, N_HEADS, 2, HEAD_DIM), k_all, v_all, mask_s, lam)
            xs = xs + attn_out(os_, subln[j], lam_init, w_o[j])
            nk_p.append(kp); nv_p.append(vp); nk_s.append(ks_); nv_s.append(vs)
        xp = xp + ffn(rmsnorm(xp, norm_ffn[i]), w_up[i], w_down[i])
        xs = xs + ffn(rmsnorm(xs, norm_ffn[i]), w_up[i], w_down[i])
    new_k_prompt = jnp.stack(nk_p)
    new_v_prompt = jnp.stack(nv_p)
    new_k_sample = jnp.stack(nk_s)
    new_v_sample = jnp.stack(nv_s)
    new_pool_prompt = jnp.stack(np_p)
    new_pool_sample = jnp.stack(np_s)
    return (xp, xs, new_k_prompt, new_v_prompt, new_k_sample, new_v_sample, new_pool_prompt, new_pool_sample)
```

```python
import functools
import math

import jax
import jax.numpy as jnp
from jax import lax
from jax.experimental import pallas as pl
from jax.experimental.pallas import tpu as pltpu

F32 = jnp.float32
BF16 = jnp.bfloat16

N_MIXERS = 2
POOL_WINDOWS = (2, 4, 8, 16)
POOL_BUF = max(POOL_WINDOWS) - 1
POOL_HALO = 16
HEAD_DIM = 64
HALF = HEAD_DIM // 2
ROPE_THETA = 10000.0
EPS = 1e-6
SUBLN_EPS = 1e-5
PAGE_SIZE = 128

V7X_VMEM_LIMIT_BYTES = 56 * 1024 * 1024


def _cparams(*sem):
    return pltpu.CompilerParams(dimension_semantics=sem, vmem_limit_bytes=V7X_VMEM_LIMIT_BYTES)


def _rms(x, g, eps=EPS):
    ms = jnp.mean(x * x, axis=-1, keepdims=True)
    return x * lax.rsqrt(ms + eps) * g


def _ffn_kernel(x_ref, g_ref, wu_ref, wd_ref, o_ref, h_scr):
    c = pl.program_id(1)

    @pl.when(c == 0)
    def _():
        x = x_ref[...]
        h_scr[...] = _rms(x, g_ref[...]).astype(BF16)
        o_ref[...] = x

    u = jnp.dot(h_scr[...], wu_ref[...], preferred_element_type=F32)
    a = jnp.maximum(u, 0.0)
    a = (a * a).astype(BF16)
    o_ref[...] += jnp.dot(a, wd_ref[...], preferred_element_type=F32)


def _ffn(x2d, g, w_up, w_down, layer, *, tm, tf):
    n, d = x2d.shape
    dff = w_up.shape[-1]
    return pl.pallas_call(
        _ffn_kernel,
        grid=(n // tm, dff // tf),
        in_specs=[
            pl.BlockSpec((tm, d), lambda i, c: (i, 0)),
            pl.BlockSpec((None, 1, d), lambda i, c: (layer, 0, 0)),
            pl.BlockSpec((None, d, tf), lambda i, c: (layer, 0, c)),
            pl.BlockSpec((None, tf, d), lambda i, c: (layer, c, 0)),
        ],
        out_specs=pl.BlockSpec((tm, d), lambda i, c: (i, 0)),
        out_shape=jax.ShapeDtypeStruct((n, d), F32),
        scratch_shapes=[pltpu.VMEM((tm, d), BF16)],
        compiler_params=_cparams("parallel", "arbitrary"),
        name="ffn",
    )(x2d, g, w_up, w_down)


def _pool_prompt_kernel(x_ref, prev_ref, g_ref, wp_ref, sc_ref, o_ref, np_ref, ext_scr, *, tm):
    i = pl.program_id(1)
    g = g_ref[...]
    x = x_ref[0]
    h = _rms(x, g)
    hp = _rms(prev_ref[0], g)
    hp = jnp.where(i == 0, 0.0, hp)
    ext_scr[0:POOL_HALO, :] = hp
    ext_scr[POOL_HALO:POOL_HALO + tm, :] = h
    pos1 = i * tm + lax.broadcasted_iota(jnp.int32, (tm, 1), 0) + 1
    gw = x.shape[1] // len(POOL_WINDOWS)
    for gi, w in enumerate(POOL_WINDOWS):
        lo = gi * gw
        hg = h[:, lo:lo + gw]
        s = hg
        for j in range(1, w):
            s = s + ext_scr[POOL_HALO - j:POOL_HALO - j + tm, lo:lo + gw]
        cnt = jnp.minimum(pos1, w).astype(F32)
        dlt = (s / cnt - hg).astype(BF16)
        y = jnp.dot(dlt, wp_ref[gi], preferred_element_type=F32)
        o_ref[0, :, lo:lo + gw] = x[:, lo:lo + gw] + y * sc_ref[:, lo:lo + gw]

    @pl.when(i == pl.num_programs(1) - 1)
    def _():
        np_ref[0] = ext_scr[tm:tm + POOL_HALO, :]


def _pool_prompt(x, g_all, w_pool, scale_all, layer, j, *, tm):
    b, t, d = x.shape
    r = tm // POOL_HALO
    n_groups = len(POOL_WINDOWS)
    gw = d // n_groups
    return pl.pallas_call(
        functools.partial(_pool_prompt_kernel, tm=tm),
        grid=(b, t // tm),
        in_specs=[
            pl.BlockSpec((1, tm, d), lambda bi, i: (bi, i, 0)),
            pl.BlockSpec((1, POOL_HALO, d), lambda bi, i: (bi, jnp.maximum(i * r - 1, 0), 0)),
            pl.BlockSpec((None, 1, d), lambda bi, i: (layer, 0, 0)),
            pl.BlockSpec((None, n_groups, gw, gw), lambda bi, i: (j, 0, 0, 0)),
            pl.BlockSpec((None, 1, d), lambda bi, i: (j, 0, 0)),
        ],
        out_specs=[
            pl.BlockSpec((1, tm, d), lambda bi, i: (bi, i, 0)),
            pl.BlockSpec((1, POOL_HALO, d), lambda bi, i: (bi, 0, 0)),
        ],
        out_shape=[
            jax.ShapeDtypeStruct((b, t, d), F32),
            jax.ShapeDtypeStruct((b, POOL_HALO, d), F32),
        ],
        scratch_shapes=[pltpu.VMEM((POOL_HALO + tm, d), F32)],
        compiler_params=_cparams("parallel", "arbitrary"),
        name="pool_prompt",
    )(x, x, g_all, w_pool, scale_all)


def _pool_sample_kernel(x_ref, st_ref, g_ref, wp_ref, sc_ref, o_ref, np_ref, *, pos0):
    ts = x_ref.shape[0]
    g = g_ref[...]
    xs = [x_ref[t] for t in range(ts)]
    ext = [st_ref[r] for r in range(POOL_BUF)] + [_rms(xt, g) for xt in xs]
    gw = xs[0].shape[1] // len(POOL_WINDOWS)
    for t in range(ts):
        e = POOL_BUF + t
        for gi, w in enumerate(POOL_WINDOWS):
            lo = gi * gw
            s = ext[e][:, lo:lo + gw]
            for jj in range(1, w):
                s = s + ext[e - jj][:, lo:lo + gw]
            cnt = float(min(pos0 + t + 1, w))
            dlt = (s / cnt - ext[e][:, lo:lo + gw]).astype(BF16)
            y = jnp.dot(dlt, wp_ref[gi], preferred_element_type=F32)
            o_ref[t, :, lo:lo + gw] = xs[t][:, lo:lo + gw] + y * sc_ref[:, lo:lo + gw]
    for r in range(POOL_BUF):
        np_ref[r] = ext[ts + r]


def _pool_sample(x_tm, state_tm, g_all, w_pool, scale_all, layer, j, *, pos0, bb):
    ts, b, d = x_tm.shape
    n_groups = len(POOL_WINDOWS)
    gw = d // n_groups
    return pl.pallas_call(
        functools.partial(_pool_sample_kernel, pos0=pos0),
        grid=(b // bb,),
        in_specs=[
            pl.BlockSpec((ts, bb, d), lambda i: (0, i, 0)),
            pl.BlockSpec((None, POOL_BUF, bb, d), lambda i: (j, 0, i, 0)),
            pl.BlockSpec((None, 1, d), lambda i: (layer, 0, 0)),
            pl.BlockSpec((None, n_groups, gw, gw), lambda i: (j, 0, 0, 0)),
            pl.BlockSpec((None, 1, d), lambda i: (j, 0, 0)),
        ],
        out_specs=[
            pl.BlockSpec((ts, bb, d), lambda i: (0, i, 0)),
            pl.BlockSpec((POOL_BUF, bb, d), lambda i: (0, i, 0)),
        ],
        out_shape=[
            jax.ShapeDtypeStruct((ts, b, d), F32),
            jax.ShapeDtypeStruct((POOL_BUF, b, d), F32),
        ],
        compiler_params=_cparams("parallel"),
        name="pool_sample",
    )(x_tm, state_tm, g_all, w_pool, scale_all)


def _qkv_kernel(x_ref, g_ref, wqk_ref, wv_ref, gq_ref, gk_ref, cos_ref, sin_ref,
                qT_ref, kT32_ref, kT16_ref, v32_ref, v16_ref, *, q_scale):
    x = x_ref[0]
    tm, d = x.shape
    h = _rms(x, g_ref[...]).astype(BF16)
    v = jnp.dot(h, wv_ref[...], preferred_element_type=F32)
    v32_ref[0] = v
    v16_ref[0] = v.astype(BF16)
    cos = cos_ref[...][None]
    sin = sin_ref[...][None]
    rows = 4 * HEAD_DIM
    for part in range(2):
        gn = (gq_ref if part == 0 else gk_ref)[...][None]
        for c in range(d // rows):
            r0 = part * d + c * rows
            yT = lax.dot_general(wqk_ref[r0:r0 + rows, :], h, (((1,), (1,)), ((), ())),
                                 preferred_element_type=F32)
            y3 = yT.reshape(rows // HEAD_DIM, HEAD_DIM, tm)
            ms = jnp.mean(y3 * y3, axis=1, keepdims=True)
            yn = y3 * lax.rsqrt(ms + EPS) * gn
            x1 = yn[:, :HALF, :]
            x2 = yn[:, HALF:, :]
            out = jnp.concatenate([x1 * cos - x2 * sin, x2 * cos + x1 * sin], axis=1)
            out = out.reshape(rows, tm)
            if part == 0:
                qT_ref[0, c * rows:(c + 1) * rows, :] = out * q_scale
            else:
                kT32_ref[0, c * rows:(c + 1) * rows, :] = out
                kT16_ref[0, c * rows:(c + 1) * rows, :] = out.astype(BF16)


def _qkv(x, g_all, wqkT, wv, gq, gk, cosT, sinT, layer, j, tab_map, *, tm):
    b, t, d = x.shape
    tok = lambda bi, i: (bi, i, 0)
    tr = lambda bi, i: (bi, 0, i)
    return pl.pallas_call(
        functools.partial(_qkv_kernel, q_scale=HEAD_DIM ** -0.5),
        grid=(b, t // tm),
        in_specs=[
            pl.BlockSpec((1, tm, d), tok),
            pl.BlockSpec((None, 1, d), lambda bi, i: (layer, 0, 0)),
            pl.BlockSpec((None, 2 * d, d), lambda bi, i: (j, 0, 0)),
            pl.BlockSpec((None, d, d), lambda bi, i: (j, 0, 0)),
            pl.BlockSpec((None, HEAD_DIM, 1), lambda bi, i: (j, 0, 0)),
            pl.BlockSpec((None, HEAD_DIM, 1), lambda bi, i: (j, 0, 0)),
            pl.BlockSpec((HALF, tm), tab_map),
            pl.BlockSpec((HALF, tm), tab_map),
        ],
        out_specs=[
            pl.BlockSpec((1, d, tm), tr),
            pl.BlockSpec((1, d, tm), tr),
            pl.BlockSpec((1, d, tm), tr),
            pl.BlockSpec((1, tm, d), tok),
            pl.BlockSpec((1, tm, d), tok),
        ],
        out_shape=[
            jax.ShapeDtypeStruct((b, d, t), F32),
            jax.ShapeDtypeStruct((b, d, t), F32),
            jax.ShapeDtypeStruct((b, d, t), BF16),
            jax.ShapeDtypeStruct((b, t, d), F32),
            jax.ShapeDtypeStruct((b, t, d), BF16),
        ],
        compiler_params=_cparams("parallel", "parallel"),
        name="qkv",
    )(x, g_all, wqkT, wv, gq, gk, cosT, sinT)


def _lambda_full(lam_ref, lam_init):
    lp = lam_ref[...]
    a = jnp.sum(lp[0:1] * lp[1:2], axis=-1, keepdims=True)
    b = jnp.sum(lp[2:3] * lp[3:4], axis=-1, keepdims=True)
    return jnp.exp(a) - jnp.exp(b) + lam_init


def _diff_combine(o1, l1, o2, l2, lam, subln, lam_init):
    o = o1 / l1 - lam * (o2 / l2)
    return _rms(o, subln, SUBLN_EPS) * (1.0 - lam_init)


def _attn_kernel(qT_ref, kT_ref, v_ref, lam_ref, subln_ref, o_ref, qs_scr, m_scr, acc_scr,
                 *, tq, rc, lam_init):
    qi = pl.program_id(2)
    vd = v_ref.shape[2]
    q = qT_ref[0].T
    lane = lax.broadcasted_iota(jnp.int32, q.shape, 1)
    qs_scr[0:tq, :] = jnp.where(lane < HEAD_DIM, q, 0.0).astype(BF16)
    qs_scr[tq:2 * tq, :] = jnp.where(lane >= HEAD_DIM, q, 0.0).astype(BF16)
    m_scr[...] = jnp.full(m_scr.shape, -jnp.inf, F32)
    acc_scr[...] = jnp.zeros(acc_scr.shape, F32)
    ones = jnp.ones((tq, vd), BF16)

    def step(jb, masked):
        k0 = pl.multiple_of(jb * tq, tq)
        kT = kT_ref[0, :, pl.ds(k0, tq)]
        vv = jnp.concatenate([v_ref[0, pl.ds(k0, tq), :], ones], axis=1)
        for r in range(0, 2 * tq, rc):
            s = jnp.dot(qs_scr[r:r + rc, :], kT, preferred_element_type=F32)
            if masked:
                rowp = (r % tq) + lax.broadcasted_iota(jnp.int32, (rc, tq), 0)
                colp = lax.broadcasted_iota(jnp.int32, (rc, tq), 1)
                s = jnp.where(colp <= rowp, s, -jnp.inf)
            m_old = m_scr[r:r + rc, :]
            m_new = jnp.maximum(m_old, jnp.max(s, axis=1, keepdims=True))
            alpha = jnp.exp(m_old - m_new)
            p = jnp.exp(s - m_new)
            pv = jnp.dot(p.astype(BF16), vv, preferred_element_type=F32)
            acc_scr[r:r + rc, :] = acc_scr[r:r + rc, :] * alpha + pv
            m_scr[r:r + rc, :] = m_new

    def body(jb, carry):
        step(jb, False)
        return carry

    lax.fori_loop(0, qi, body, 0)
    step(qi, True)

    lam = _lambda_full(lam_ref, lam_init)
    acc = acc_scr[...]
    o = _diff_combine(acc[0:tq, 0:vd], acc[0:tq, vd:vd + 1], acc[tq:, 0:vd], acc[tq:, vd:vd + 1],
                      lam, subln_ref[...], lam_init)
    o_ref[0] = o.astype(BF16)


def _attn_prompt(qT, kT16, v16, lam_params, subln_all, j, lam_init, *, tq, rc):
    b, d, t = qT.shape
    vd = 2 * HEAD_DIM
    n_heads = d // vd
    return pl.pallas_call(
        functools.partial(_attn_kernel, tq=tq, rc=rc, lam_init=lam_init),
        grid=(b, n_heads, t // tq),
        in_specs=[
            pl.BlockSpec((1, vd, tq), lambda bi, hi, qi: (bi, hi, qi)),
            pl.BlockSpec((1, vd, t), lambda bi, hi, qi: (bi, hi, 0)),
            pl.BlockSpec((1, t, vd), lambda bi, hi, qi: (bi, 0, hi)),
            pl.BlockSpec((None, 4, HEAD_DIM), lambda bi, hi, qi: (j, 0, 0)),
            pl.BlockSpec((None, 1, vd), lambda bi, hi, qi: (j, 0, 0)),
        ],
        out_specs=pl.BlockSpec((1, tq, vd), lambda bi, hi, qi: (bi, qi, hi)),
        out_shape=jax.ShapeDtypeStruct((b, t, d), BF16),
        scratch_shapes=[
            pltpu.VMEM((2 * tq, vd), BF16),
            pltpu.VMEM((2 * tq, 1), F32),
            pltpu.VMEM((2 * tq, 2 * vd), F32),
        ],
        compiler_params=_cparams("parallel", "parallel", "arbitrary"),
        name="attn_prompt",
    )(qT, kT16, v16, lam_params, subln_all)


def _decode_kernel(pt_ref, q_ref, kn_ref, vn_ref, lam_ref, subln_ref, *rest, n_pages, lam_init):
    del pt_ref
    k_refs = rest[:n_pages]
    v_refs = rest[n_pages:2 * n_pages]
    o_ref = rest[2 * n_pages]
    q = q_ref[0]
    kn = kn_ref[0]
    vn = vn_ref[0]
    ts = q.shape[0]
    vd = 2 * HEAD_DIM
    causal = (lax.broadcasted_iota(jnp.int32, (ts, ts), 1)
              <= lax.broadcasted_iota(jnp.int32, (ts, ts), 0))
    lam = _lambda_full(lam_ref, lam_init)
    subln = subln_ref[...]
    nt = (((1,), (1,)), ((), ()))
    for h in range(q.shape[1] // vd):
        probs = []
        for c in range(2):
            hc = 2 * h + c
            qh = q[:, hc * HEAD_DIM:(hc + 1) * HEAD_DIM]
            s_pg = [jnp.dot(qh, k_refs[p][0, 0, hc], preferred_element_type=F32)
                    for p in range(n_pages)]
            s_new = lax.dot_general(qh, kn[:, hc * HEAD_DIM:(hc + 1) * HEAD_DIM], nt,
                                    preferred_element_type=F32)
            s_new = jnp.where(causal, s_new, -jnp.inf)
            m_el = s_pg[0]
            for p in range(1, n_pages):
                m_el = jnp.maximum(m_el, s_pg[p])
            m = jnp.maximum(jnp.max(m_el, axis=1, keepdims=True),
                            jnp.max(s_new, axis=1, keepdims=True))
            p_pg = [jnp.exp(s - m) for s in s_pg]
            p_new = jnp.exp(s_new - m)
            l_el = p_pg[0]
            for p in range(1, n_pages):
                l_el = l_el + p_pg[p]
            l = jnp.sum(l_el, axis=1, keepdims=True) + jnp.sum(p_new, axis=1, keepdims=True)
            probs.append((p_pg, p_new, l))
        acc = jnp.dot(jnp.concatenate([probs[0][1], probs[1][1]], axis=0),
                      vn[:, h * vd:(h + 1) * vd], preferred_element_type=F32)
        for p in range(n_pages):
            pp = jnp.concatenate([probs[0][0][p], probs[1][0][p]], axis=0)
            acc = acc + jnp.dot(pp, v_refs[p][0, 0, :, h, :], preferred_element_type=F32)
        o = _diff_combine(acc[:ts], probs[0][2], acc[ts:], probs[1][2], lam, subln, lam_init)
        o_ref[0, :, h * vd:(h + 1) * vd] = o.astype(BF16)


def _attn_decode(page_table_flat, q, kn, vn, lam_params, subln_all, cache_kT, cache_v, j, lam_init):
    b, ts, d = q.shape
    vd = 2 * HEAD_DIM
    n_pages = page_table_flat.shape[0] // b
    n_hc = cache_kT.shape[2]
    n_heads = cache_v.shape[3]
    row = lambda bi, pt: (bi, 0, 0)

    def page_map(p):
        return lambda bi, pt: (j, pt[bi * n_pages + p], 0, 0, 0)

    k_specs = [pl.BlockSpec((1, 1, n_hc, HEAD_DIM, PAGE_SIZE), page_map(p)) for p in range(n_pages)]
    v_specs = [pl.BlockSpec((1, 1, PAGE_SIZE, n_heads, vd), page_map(p)) for p in range(n_pages)]
    grid_spec = pltpu.PrefetchScalarGridSpec(
        num_scalar_prefetch=1,
        grid=(b,),
        in_specs=[
            pl.BlockSpec((1, ts, d), row),
            pl.BlockSpec((1, ts, d), row),
            pl.BlockSpec((1, ts, d), row),
            pl.BlockSpec((None, 4, HEAD_DIM), lambda bi, pt: (j, 0, 0)),
            pl.BlockSpec((None, 1, vd), lambda bi, pt: (j, 0, 0)),
        ] + k_specs + v_specs,
        out_specs=pl.BlockSpec((1, ts, d), row),
    )
    return pl.pallas_call(
        functools.partial(_decode_kernel, n_pages=n_pages, lam_init=lam_init),
        grid_spec=grid_spec,
        out_shape=jax.ShapeDtypeStruct((b, ts, d), BF16),
        compiler_params=_cparams("arbitrary"),
        name="attn_decode",
    )(page_table_flat, q, kn, vn, lam_params, subln_all, *([cache_kT] * n_pages), *([cache_v] * n_pages))


def _proj_kernel(x_ref, o_ref, w_ref, y_ref):
    y_ref[...] = x_ref[...] + jnp.dot(o_ref[...], w_ref[...], preferred_element_type=F32)


def _proj_res(x2d, o2d, w_o, j, *, tm):
    n, d = x2d.shape
    return pl.pallas_call(
        _proj_kernel,
        grid=(n // tm,),
        in_specs=[
            pl.BlockSpec((tm, d), lambda i: (i, 0)),
            pl.BlockSpec((tm, d), lambda i: (i, 0)),
            pl.BlockSpec((None, d, d), lambda i: (j, 0, 0)),
        ],
        out_specs=pl.BlockSpec((tm, d), lambda i: (i, 0)),
        out_shape=jax.ShapeDtypeStruct((n, d), F32),
        compiler_params=_cparams("parallel"),
        name="proj_res",
    )(x2d, o2d, w_o)


def _rope_tables(pos):
    inv = ROPE_THETA ** (-jnp.arange(HALF, dtype=F32) / HALF)
    ang = inv[:, None] * pos.astype(F32)[None, :]
    return jnp.cos(ang), jnp.sin(ang)


def kernel(x_prompt, x_sample, cache_k, cache_v, page_table, state_pool, norm_mix, norm_ffn, w_pool,
           pool_scale, w_qkv, q_norm, k_norm, lambda_q1, lambda_k1, lambda_q2, lambda_k2, subln, w_o,
           w_up, w_down):
    bp, sp, d = x_prompt.shape
    bs, ts, _ = x_sample.shape
    depth = norm_mix.shape[0]
    past_len = page_table.shape[1] * PAGE_SIZE
    n_hc = d // HEAD_DIM
    n_heads = n_hc // 2
    vd = 2 * HEAD_DIM

    w_up16 = w_up.astype(BF16)
    w_down16 = w_down.astype(BF16)
    w_pool16 = w_pool.astype(BF16)
    w_o16 = w_o.astype(BF16)
    wqkT16 = jnp.swapaxes(w_qkv[:, :, :2 * d], 1, 2).astype(BF16)
    wv16 = w_qkv[:, :, 2 * d:].astype(BF16)
    gq = q_norm[:, :, None]
    gk = k_norm[:, :, None]
    lam_params = jnp.stack([lambda_q1, lambda_k1, lambda_q2, lambda_k2], axis=1)
    norm_mix = norm_mix[:, None, :]
    norm_ffn = norm_ffn[:, None, :]
    pool_scale = pool_scale[:, None, :]
    subln = subln[:, None, :]

    cos_p, sin_p = _rope_tables(jnp.arange(sp))
    cos_s, sin_s = _rope_tables(past_len + jnp.tile(jnp.arange(ts), bs))

    cache_kT = jnp.transpose(cache_k, (0, 1, 3, 4, 2))
    state_tm = jnp.transpose(state_pool, (0, 2, 1, 3))
    pt_flat = page_table.reshape(-1)

    xp = x_prompt
    xs = x_sample.reshape(bs * ts, d)
    nk_p, nv_p, nk_s, nv_s, np_p, np_s = [], [], [], [], [], []
    for i in range(depth):
        j = i // N_MIXERS
        if i % N_MIXERS == 0:
            xp, pool_p = _pool_prompt(xp, norm_mix, w_pool16, pool_scale, i, j, tm=512)
            np_p.append(pool_p[:, POOL_HALO - POOL_BUF:])
            xs_tm = jnp.swapaxes(xs.reshape(bs, ts, d), 0, 1)
            xs_tm, pool_s = _pool_sample(xs_tm, state_tm, norm_mix, w_pool16, pool_scale, i, j,
                                         pos0=past_len, bb=32)
            xs = jnp.swapaxes(xs_tm, 0, 1).reshape(bs * ts, d)
            np_s.append(pool_s)
        else:
            lam_init = 0.8 - 0.6 * math.exp(-0.3 * i)
            qT, kT32, kT16, v32, v16 = _qkv(xp, norm_mix, wqkT16, wv16, gq, gk, cos_p, sin_p, i, j,
                                            lambda bi, ti: (0, ti), tm=512)
            o_p = _attn_prompt(qT, kT16, v16, lam_params, subln, j, lam_init, tq=512, rc=256)
            xp = _proj_res(xp.reshape(bp * sp, d), o_p.reshape(bp * sp, d), w_o16, j,
                           tm=1024).reshape(bp, sp, d)
            nk_p.append(kT32.reshape(bp, n_hc, HEAD_DIM, sp))
            nv_p.append(v32.reshape(bp, sp, n_heads, vd))
            qT_s, kT32_s, _, v32_s, _ = _qkv(xs[None], norm_mix, wqkT16, wv16, gq, gk, cos_s, sin_s,
                                             i, j, lambda bi, ti: (0, ti), tm=512)
            q_s = qT_s[0].T.reshape(bs, ts, d)
            kn_s = kT32_s[0].T.reshape(bs, ts, d)
            vn_s = v32_s.reshape(bs, ts, d)
            o_s = _attn_decode(pt_flat, q_s, kn_s, vn_s, lam_params, subln, cache_kT, cache_v, j,
                               lam_init)
            xs = _proj_res(xs, o_s.reshape(bs * ts, d), w_o16, j, tm=1024)
            nk_s.append(kn_s.reshape(bs, ts, n_hc, HEAD_DIM))
            nv_s.append(vn_s.reshape(bs, ts, n_heads, vd))
        xp = _ffn(xp.reshape(bp * sp, d), norm_ffn, w_up16, w_down16, i, tm=1024,
                  tf=1024).reshape(bp, sp, d)
        xs = _ffn(xs, norm_ffn, w_up16, w_down16, i, tm=1024, tf=1024)

    new_k_prompt = jnp.transpose(jnp.stack(nk_p), (0, 1, 4, 2, 3))
    new_v_prompt = jnp.stack(nv_p)
    new_k_sample = jnp.stack(nk_s)
    new_v_sample = jnp.stack(nv_s)
    new_pool_prompt = jnp.stack(np_p)
    new_pool_sample = jnp.transpose(jnp.stack(np_s), (0, 2, 1, 3))
    return (xp, xs.reshape(bs, ts, d), new_k_prompt, new_v_prompt, new_k_sample, new_v_sample,
            new_pool_prompt, new_pool_sample)
```

```python
import functools
import math

import jax
import jax.numpy as jnp
from jax import lax
from jax.experimental import pallas as pl
from jax.experimental.pallas import tpu as pltpu

F32 = jnp.float32
BF16 = jnp.bfloat16

N_MIXERS = 2
POOL_WINDOWS = (2, 4, 8, 16)
POOL_BUF = max(POOL_WINDOWS) - 1
POOL_HALO = 16
HEAD_DIM = 64
HALF = HEAD_DIM // 2
ROPE_THETA = 10000.0
EPS = 1e-6
SUBLN_EPS = 1e-5
PAGE_SIZE = 128
SCORE_BOUND_LOG2 = 48.0
Q_SCALE = math.log2(math.e) * HEAD_DIM ** -0.5

V7X_VMEM_LIMIT_BYTES = 56 * 1024 * 1024


def _cparams(*sem):
    return pltpu.CompilerParams(dimension_semantics=sem, vmem_limit_bytes=V7X_VMEM_LIMIT_BYTES)


def _rms(x, g, eps=EPS):
    ms = jnp.mean(x * x, axis=-1, keepdims=True)
    return x * lax.rsqrt(ms + eps) * g


def _ffn_kernel(x_ref, g_ref, wu_ref, wd_ref, o_ref, h_scr):
    c = pl.program_id(1)

    @pl.when(c == 0)
    def _():
        x = x_ref[...]
        h_scr[...] = _rms(x, g_ref[...]).astype(BF16)
        o_ref[...] = x

    u = jnp.dot(h_scr[...], wu_ref[...], preferred_element_type=F32)
    a = jnp.maximum(u, 0.0)
    a = (a * a).astype(BF16)
    o_ref[...] += jnp.dot(a, wd_ref[...], preferred_element_type=F32)


def _ffn(x2d, g, w_up, w_down, layer, *, tm, tf):
    n, d = x2d.shape
    dff = w_up.shape[-1]
    return pl.pallas_call(
        _ffn_kernel,
        grid=(n // tm, dff // tf),
        in_specs=[
            pl.BlockSpec((tm, d), lambda i, c: (i, 0)),
            pl.BlockSpec((None, 1, d), lambda i, c: (layer, 0, 0)),
            pl.BlockSpec((None, d, tf), lambda i, c: (layer, 0, c)),
            pl.BlockSpec((None, tf, d), lambda i, c: (layer, c, 0)),
        ],
        out_specs=pl.BlockSpec((tm, d), lambda i, c: (i, 0)),
        out_shape=jax.ShapeDtypeStruct((n, d), F32),
        scratch_shapes=[pltpu.VMEM((tm, d), BF16)],
        compiler_params=_cparams("parallel", "arbitrary"),
        name="ffn",
    )(x2d, g, w_up, w_down)


def _pool_prompt_kernel(x_ref, prev_ref, g_ref, wp_ref, sc_ref, o_ref, np_ref, ext_scr, *, tm):
    i = pl.program_id(1)
    g = g_ref[...]
    x = x_ref[0]
    h = _rms(x, g)
    hp = _rms(prev_ref[0], g)
    hp = jnp.where(i == 0, 0.0, hp)
    ext_scr[0:POOL_HALO, :] = hp
    ext_scr[POOL_HALO:POOL_HALO + tm, :] = h
    pos1 = i * tm + lax.broadcasted_iota(jnp.int32, (tm, 1), 0) + 1
    gw = x.shape[1] // len(POOL_WINDOWS)
    for gi, w in enumerate(POOL_WINDOWS):
        lo = gi * gw
        hg = h[:, lo:lo + gw]
        s = hg
        for j in range(1, w):
            s = s + ext_scr[POOL_HALO - j:POOL_HALO - j + tm, lo:lo + gw]
        cnt = jnp.minimum(pos1, w).astype(F32)
        dlt = (s / cnt - hg).astype(BF16)
        y = jnp.dot(dlt, wp_ref[gi], preferred_element_type=F32)
        o_ref[0, :, lo:lo + gw] = x[:, lo:lo + gw] + y * sc_ref[:, lo:lo + gw]

    @pl.when(i == pl.num_programs(1) - 1)
    def _():
        np_ref[0] = ext_scr[tm:tm + POOL_HALO, :]


def _pool_prompt(x, g_all, w_pool, scale_all, layer, j, *, tm):
    b, t, d = x.shape
    r = tm // POOL_HALO
    n_groups = len(POOL_WINDOWS)
    gw = d // n_groups
    return pl.pallas_call(
        functools.partial(_pool_prompt_kernel, tm=tm),
        grid=(b, t // tm),
        in_specs=[
            pl.BlockSpec((1, tm, d), lambda bi, i: (bi, i, 0)),
            pl.BlockSpec((1, POOL_HALO, d), lambda bi, i: (bi, jnp.maximum(i * r - 1, 0), 0)),
            pl.BlockSpec((None, 1, d), lambda bi, i: (layer, 0, 0)),
            pl.BlockSpec((None, n_groups, gw, gw), lambda bi, i: (j, 0, 0, 0)),
            pl.BlockSpec((None, 1, d), lambda bi, i: (j, 0, 0)),
        ],
        out_specs=[
            pl.BlockSpec((1, tm, d), lambda bi, i: (bi, i, 0)),
            pl.BlockSpec((1, POOL_HALO, d), lambda bi, i: (bi, 0, 0)),
        ],
        out_shape=[
            jax.ShapeDtypeStruct((b, t, d), F32),
            jax.ShapeDtypeStruct((b, POOL_HALO, d), F32),
        ],
        scratch_shapes=[pltpu.VMEM((POOL_HALO + tm, d), F32)],
        compiler_params=_cparams("parallel", "arbitrary"),
        name="pool_prompt",
    )(x, x, g_all, w_pool, scale_all)


def _pool_sample_kernel(x_ref, st_ref, g_ref, wp_ref, sc_ref, o_ref, np_ref, *, pos0):
    ts = x_ref.shape[0]
    g = g_ref[...]
    xs = [x_ref[t] for t in range(ts)]
    ext = [st_ref[r] for r in range(POOL_BUF)] + [_rms(xt, g) for xt in xs]
    gw = xs[0].shape[1] // len(POOL_WINDOWS)
    for t in range(ts):
        e = POOL_BUF + t
        for gi, w in enumerate(POOL_WINDOWS):
            lo = gi * gw
            s = ext[e][:, lo:lo + gw]
            for jj in range(1, w):
                s = s + ext[e - jj][:, lo:lo + gw]
            cnt = float(min(pos0 + t + 1, w))
            dlt = (s / cnt - ext[e][:, lo:lo + gw]).astype(BF16)
            y = jnp.dot(dlt, wp_ref[gi], preferred_element_type=F32)
            o_ref[t, :, lo:lo + gw] = xs[t][:, lo:lo + gw] + y * sc_ref[:, lo:lo + gw]
    for r in range(POOL_BUF):
        np_ref[r] = ext[ts + r]


def _pool_sample(x_tm, state_tm, g_all, w_pool, scale_all, layer, j, *, pos0, bb):
    ts, b, d = x_tm.shape
    n_groups = len(POOL_WINDOWS)
    gw = d // n_groups
    return pl.pallas_call(
        functools.partial(_pool_sample_kernel, pos0=pos0),
        grid=(b // bb,),
        in_specs=[
            pl.BlockSpec((ts, bb, d), lambda i: (0, i, 0)),
            pl.BlockSpec((None, POOL_BUF, bb, d), lambda i: (j, 0, i, 0)),
            pl.BlockSpec((None, 1, d), lambda i: (layer, 0, 0)),
            pl.BlockSpec((None, n_groups, gw, gw), lambda i: (j, 0, 0, 0)),
            pl.BlockSpec((None, 1, d), lambda i: (j, 0, 0)),
        ],
        out_specs=[
            pl.BlockSpec((ts, bb, d), lambda i: (0, i, 0)),
            pl.BlockSpec((POOL_BUF, bb, d), lambda i: (0, i, 0)),
        ],
        out_shape=[
            jax.ShapeDtypeStruct((ts, b, d), F32),
            jax.ShapeDtypeStruct((POOL_BUF, b, d), F32),
        ],
        compiler_params=_cparams("parallel"),
        name="pool_sample",
    )(x_tm, state_tm, g_all, w_pool, scale_all)


def _qkv_kernel(x_ref, g_ref, wqk_ref, wv_ref, gq_ref, gk_ref, cos_ref, sin_ref,
                qT_ref, kT32_ref, kT16_ref, v32_ref, v16_ref, *, q_scale):
    x = x_ref[0]
    tm, d = x.shape
    h = _rms(x, g_ref[...]).astype(BF16)
    v = jnp.dot(h, wv_ref[...], preferred_element_type=F32)
    v32_ref[0] = v
    v16_ref[0] = v.astype(BF16)
    cos = cos_ref[...][None]
    sin = sin_ref[...][None]
    rows = 4 * HEAD_DIM
    for part in range(2):
        gn = (gq_ref if part == 0 else gk_ref)[...][None]
        for c in range(d // rows):
            r0 = part * d + c * rows
            yT = lax.dot_general(wqk_ref[r0:r0 + rows, :], h, (((1,), (1,)), ((), ())),
                                 preferred_element_type=F32)
            y3 = yT.reshape(rows // HEAD_DIM, HEAD_DIM, tm)
            ms = jnp.mean(y3 * y3, axis=1, keepdims=True)
            yn = y3 * lax.rsqrt(ms + EPS) * gn
            x1 = yn[:, :HALF, :]
            x2 = yn[:, HALF:, :]
            out = jnp.concatenate([x1 * cos - x2 * sin, x2 * cos + x1 * sin], axis=1)
            out = out.reshape(rows, tm)
            if part == 0:
                qT_ref[0, c * rows:(c + 1) * rows, :] = out * q_scale
            else:
                kT32_ref[0, c * rows:(c + 1) * rows, :] = out
                kT16_ref[0, c * rows:(c + 1) * rows, :] = out.astype(BF16)


def _qkv(x, g_all, wqkT, wv, gq, gk, cosT, sinT, layer, j, tab_map, *, tm):
    b, t, d = x.shape
    tok = lambda bi, i: (bi, i, 0)
    tr = lambda bi, i: (bi, 0, i)
    return pl.pallas_call(
        functools.partial(_qkv_kernel, q_scale=Q_SCALE),
        grid=(b, t // tm),
        in_specs=[
            pl.BlockSpec((1, tm, d), tok),
            pl.BlockSpec((None, 1, d), lambda bi, i: (layer, 0, 0)),
            pl.BlockSpec((None, 2 * d, d), lambda bi, i: (j, 0, 0)),
            pl.BlockSpec((None, d, d), lambda bi, i: (j, 0, 0)),
            pl.BlockSpec((None, HEAD_DIM, 1), lambda bi, i: (j, 0, 0)),
            pl.BlockSpec((None, HEAD_DIM, 1), lambda bi, i: (j, 0, 0)),
            pl.BlockSpec((HALF, tm), tab_map),
            pl.BlockSpec((HALF, tm), tab_map),
        ],
        out_specs=[
            pl.BlockSpec((1, d, tm), tr),
            pl.BlockSpec((1, d, tm), tr),
            pl.BlockSpec((1, d, tm), tr),
            pl.BlockSpec((1, tm, d), tok),
            pl.BlockSpec((1, tm, d), tok),
        ],
        out_shape=[
            jax.ShapeDtypeStruct((b, d, t), F32),
            jax.ShapeDtypeStruct((b, d, t), F32),
            jax.ShapeDtypeStruct((b, d, t), BF16),
            jax.ShapeDtypeStruct((b, t, d), F32),
            jax.ShapeDtypeStruct((b, t, d), BF16),
        ],
        compiler_params=_cparams("parallel", "parallel"),
        name="qkv",
    )(x, g_all, wqkT, wv, gq, gk, cosT, sinT)


def _lambda_full(lam_ref, lam_init):
    lp = lam_ref[...]
    a = jnp.sum(lp[0:1] * lp[1:2], axis=-1, keepdims=True)
    b = jnp.sum(lp[2:3] * lp[3:4], axis=-1, keepdims=True)
    return jnp.exp(a) - jnp.exp(b) + lam_init


def _diff_combine(o1, l1, o2, l2, lam, subln, lam_init):
    o = o1 / l1 - lam * (o2 / l2)
    return _rms(o, subln, SUBLN_EPS) * (1.0 - lam_init)


def _attn_kernel(bounded_ref, qT_ref, kT_ref, v_ref, lam_ref, subln_ref, o_ref, qs_scr, m_scr,
                 acc_scr, *, tq, tk, rc, lam_init):
    qi = pl.program_id(2)
    vd = v_ref.shape[2]
    q = qT_ref[0].T
    lane = lax.broadcasted_iota(jnp.int32, q.shape, 1)
    qs_scr[0:tq, :] = jnp.where(lane < HEAD_DIM, q, 0.0).astype(BF16)
    qs_scr[tq:2 * tq, :] = jnp.where(lane >= HEAD_DIM, q, 0.0).astype(BF16)
    acc_scr[...] = jnp.zeros(acc_scr.shape, F32)
    ones = jnp.ones((tk, vd), BF16)
    lower_tri = (lax.broadcasted_iota(jnp.int32, (rc, rc), 1)
                 <= lax.broadcasted_iota(jnp.int32, (rc, rc), 0))

    def keys(k0, n):
        kT = kT_ref[0, :, pl.ds(k0, n)]
        vv = jnp.concatenate([v_ref[0, pl.ds(k0, n), :], ones[:n]], axis=1)
        return kT, vv

    def update(r, kT, vv, bounded, triangle=False):
        s = jnp.dot(qs_scr[r:r + rc, :], kT, preferred_element_type=F32)
        if triangle:
            s = jnp.where(lower_tri, s, -jnp.inf)
        if bounded:
            pv = jnp.dot(jnp.exp2(s).astype(BF16), vv, preferred_element_type=F32)
            acc_scr[r:r + rc, :] += pv
        else:
            m_old = m_scr[r:r + rc, :]
            m_new = jnp.maximum(m_old, jnp.max(s, axis=1, keepdims=True))
            alpha = jnp.exp2(m_old - m_new)
            p = jnp.exp2(s - m_new)
            pv = jnp.dot(p.astype(BF16), vv, preferred_element_type=F32)
            acc_scr[r:r + rc, :] = acc_scr[r:r + rc, :] * alpha + pv
            m_scr[r:r + rc, :] = m_new

    def sweep(bounded):
        def body(jb, carry):
            kT, vv = keys(pl.multiple_of(jb * tk, tk), tk)
            for r in range(0, 2 * tq, rc):
                update(r, kT, vv, bounded)
            return carry

        lax.fori_loop(0, qi * (tq // tk), body, 0)
        base = pl.multiple_of(qi * tq, tq)
        for r in range(0, 2 * tq, rc):
            off = r % tq
            for c0 in range(0, off, tk):
                kT, vv = keys(pl.multiple_of(base + c0, rc), min(tk, off - c0))
                update(r, kT, vv, bounded)
            kT, vv = keys(pl.multiple_of(base + off, rc), rc)
            update(r, kT, vv, bounded, triangle=True)

    is_bounded = bounded_ref[0] == 1

    @pl.when(is_bounded)
    def _():
        sweep(True)

    @pl.when(jnp.logical_not(is_bounded))
    def _():
        m_scr[...] = jnp.full(m_scr.shape, -jnp.inf, F32)
        sweep(False)

    lam = _lambda_full(lam_ref, lam_init)
    acc = acc_scr[...]
    o = _diff_combine(acc[0:tq, 0:vd], acc[0:tq, vd:vd + 1], acc[tq:, 0:vd], acc[tq:, vd:vd + 1],
                      lam, subln_ref[...], lam_init)
    o_ref[0] = o.astype(BF16)


def _attn_prompt(bounded, qT, kT16, v16, lam_params, subln_all, j, lam_init, *, tq, tk, rc):
    b, d, t = qT.shape
    vd = 2 * HEAD_DIM
    n_heads = d // vd
    grid_spec = pltpu.PrefetchScalarGridSpec(
        num_scalar_prefetch=1,
        grid=(b, n_heads, t // tq),
        in_specs=[
            pl.BlockSpec((1, vd, tq), lambda bi, hi, qi, fl: (bi, hi, qi)),
            pl.BlockSpec((1, vd, t), lambda bi, hi, qi, fl: (bi, hi, 0)),
            pl.BlockSpec((1, t, vd), lambda bi, hi, qi, fl: (bi, 0, hi)),
            pl.BlockSpec((None, 4, HEAD_DIM), lambda bi, hi, qi, fl: (j, 0, 0)),
            pl.BlockSpec((None, 1, vd), lambda bi, hi, qi, fl: (j, 0, 0)),
        ],
        out_specs=pl.BlockSpec((1, tq, vd), lambda bi, hi, qi, fl: (bi, qi, hi)),
        scratch_shapes=[
            pltpu.VMEM((2 * tq, vd), BF16),
            pltpu.VMEM((2 * tq, 1), F32),
            pltpu.VMEM((2 * tq, 2 * vd), F32),
        ],
    )
    return pl.pallas_call(
        functools.partial(_attn_kernel, tq=tq, tk=tk, rc=rc, lam_init=lam_init),
        grid_spec=grid_spec,
        out_shape=jax.ShapeDtypeStruct((b, t, d), BF16),
        compiler_params=_cparams("parallel", "parallel", "arbitrary"),
        name="attn_prompt",
    )(bounded, qT, kT16, v16, lam_params, subln_all)


def _scores_bounded(q_gain, k_gain):
    bound = (math.sqrt(HEAD_DIM) * math.log2(math.e)
             * jnp.max(jnp.abs(q_gain), axis=-1) * jnp.max(jnp.abs(k_gain), axis=-1))
    return (bound <= SCORE_BOUND_LOG2).astype(jnp.int32)


def _decode_kernel(pt_ref, q_ref, kn_ref, vn_ref, lam_ref, subln_ref, *rest, n_pages, lam_init):
    del pt_ref
    k_refs = rest[:n_pages]
    v_refs = rest[n_pages:2 * n_pages]
    o_ref, s_scr, p_scr = rest[2 * n_pages:]
    q = q_ref[0]
    kn = kn_ref[0]
    vn = vn_ref[0]
    ts = q.shape[0]
    vd = 2 * HEAD_DIM
    n_heads = q.shape[1] // vd
    causal = (lax.broadcasted_iota(jnp.int32, (ts, ts), 1)
              <= lax.broadcasted_iota(jnp.int32, (ts, ts), 0))
    lam = _lambda_full(lam_ref, lam_init)
    subln = subln_ref[...]
    nt = (((1,), (1,)), ((), ()))
    n_hc = 2 * n_heads
    qh = [q[:, hc * HEAD_DIM:(hc + 1) * HEAD_DIM] for hc in range(n_hc)]
    for hc in range(n_hc):
        for p in range(n_pages):
            s_scr[hc, :, p * PAGE_SIZE:(p + 1) * PAGE_SIZE] = jnp.dot(
                qh[hc], k_refs[p][0, 0, hc], preferred_element_type=F32)
    p_new, row_sum = [], []
    for hc in range(n_hc):
        s = s_scr[hc]
        s_new = lax.dot_general(qh[hc], kn[:, hc * HEAD_DIM:(hc + 1) * HEAD_DIM], nt,
                                preferred_element_type=F32)
        s_new = jnp.where(causal, s_new, -jnp.inf)
        m = jnp.maximum(jnp.max(s, axis=1, keepdims=True), jnp.max(s_new, axis=1, keepdims=True))
        pr = jnp.exp2(s - m)
        pn = jnp.exp2(s_new - m)
        p_scr[hc // 2, (hc % 2) * ts:(hc % 2 + 1) * ts, :] = pr
        p_new.append(pn)
        row_sum.append(jnp.sum(pr, axis=1, keepdims=True) + jnp.sum(pn, axis=1, keepdims=True))
    for h in range(n_heads):
        acc = jnp.dot(jnp.concatenate([p_new[2 * h], p_new[2 * h + 1]], axis=0),
                      vn[:, h * vd:(h + 1) * vd], preferred_element_type=F32)
        for p in range(n_pages):
            vh = v_refs[p][pl.ds(h, PAGE_SIZE, stride=n_heads), :]
            acc = acc + jnp.dot(p_scr[h, :, p * PAGE_SIZE:(p + 1) * PAGE_SIZE], vh,
                                preferred_element_type=F32)
        o = _diff_combine(acc[:ts], row_sum[2 * h], acc[ts:], row_sum[2 * h + 1], lam, subln,
                          lam_init)
        o_ref[0, :, h * vd:(h + 1) * vd] = o.astype(BF16)


def _attn_decode(page_table_flat, q, kn, vn, lam_params, subln_all, cache_kT, cache_v, j, lam_init):
    b, ts, d = q.shape
    vd = 2 * HEAD_DIM
    n_pages = page_table_flat.shape[0] // b
    n_hc = cache_kT.shape[2]
    row = lambda bi, pt: (bi, 0, 0)

    def k_map(p):
        return lambda bi, pt: (j, pt[bi * n_pages + p], 0, 0, 0)

    def v_map(p):
        return lambda bi, pt: (j, pt[bi * n_pages + p], 0, 0)

    k_specs = [pl.BlockSpec((1, 1, n_hc, HEAD_DIM, PAGE_SIZE), k_map(p)) for p in range(n_pages)]
    v_specs = [pl.BlockSpec((None, None, cache_v.shape[2], vd), v_map(p)) for p in range(n_pages)]
    grid_spec = pltpu.PrefetchScalarGridSpec(
        num_scalar_prefetch=1,
        grid=(b,),
        in_specs=[
            pl.BlockSpec((1, ts, d), row),
            pl.BlockSpec((1, ts, d), row),
            pl.BlockSpec((1, ts, d), row),
            pl.BlockSpec((None, 4, HEAD_DIM), lambda bi, pt: (j, 0, 0)),
            pl.BlockSpec((None, 1, vd), lambda bi, pt: (j, 0, 0)),
        ] + k_specs + v_specs,
        out_specs=pl.BlockSpec((1, ts, d), row),
        scratch_shapes=[
            pltpu.VMEM((n_hc, ts, n_pages * PAGE_SIZE), F32),
            pltpu.VMEM((n_hc // 2, 2 * ts, n_pages * PAGE_SIZE), F32),
        ],
    )
    return pl.pallas_call(
        functools.partial(_decode_kernel, n_pages=n_pages, lam_init=lam_init),
        grid_spec=grid_spec,
        out_shape=jax.ShapeDtypeStruct((b, ts, d), BF16),
        compiler_params=_cparams("arbitrary"),
        name="attn_decode",
    )(page_table_flat, q, kn, vn, lam_params, subln_all, *([cache_kT] * n_pages), *([cache_v] * n_pages))


def _proj_kernel(x_ref, o_ref, w_ref, y_ref):
    y_ref[...] = x_ref[...] + jnp.dot(o_ref[...], w_ref[...], preferred_element_type=F32)


def _proj_res(x2d, o2d, w_o, j, *, tm):
    n, d = x2d.shape
    return pl.pallas_call(
        _proj_kernel,
        grid=(n // tm,),
        in_specs=[
            pl.BlockSpec((tm, d), lambda i: (i, 0)),
            pl.BlockSpec((tm, d), lambda i: (i, 0)),
            pl.BlockSpec((None, d, d), lambda i: (j, 0, 0)),
        ],
        out_specs=pl.BlockSpec((tm, d), lambda i: (i, 0)),
        out_shape=jax.ShapeDtypeStruct((n, d), F32),
        compiler_params=_cparams("parallel"),
        name="proj_res",
    )(x2d, o2d, w_o)


def _rope_tables(pos):
    inv = ROPE_THETA ** (-jnp.arange(HALF, dtype=F32) / HALF)
    ang = inv[:, None] * pos.astype(F32)[None, :]
    return jnp.cos(ang), jnp.sin(ang)


def kernel(x_prompt, x_sample, cache_k, cache_v, page_table, state_pool, norm_mix, norm_ffn, w_pool,
           pool_scale, w_qkv, q_norm, k_norm, lambda_q1, lambda_k1, lambda_q2, lambda_k2, subln, w_o,
           w_up, w_down):
    bp, sp, d = x_prompt.shape
    bs, ts, _ = x_sample.shape
    depth = norm_mix.shape[0]
    past_len = page_table.shape[1] * PAGE_SIZE
    n_hc = d // HEAD_DIM
    n_heads = n_hc // 2
    vd = 2 * HEAD_DIM

    w_up16 = w_up.astype(BF16)
    w_down16 = w_down.astype(BF16)
    w_pool16 = w_pool.astype(BF16)
    w_o16 = w_o.astype(BF16)
    wqkT16 = jnp.swapaxes(w_qkv[:, :, :2 * d], 1, 2).astype(BF16)
    wv16 = w_qkv[:, :, 2 * d:].astype(BF16)
    gq = q_norm[:, :, None]
    gk = k_norm[:, :, None]
    lam_params = jnp.stack([lambda_q1, lambda_k1, lambda_q2, lambda_k2], axis=1)
    norm_mix = norm_mix[:, None, :]
    norm_ffn = norm_ffn[:, None, :]
    pool_scale = pool_scale[:, None, :]
    subln = subln[:, None, :]

    cos_p, sin_p = _rope_tables(jnp.arange(sp))
    cos_s, sin_s = _rope_tables(past_len + jnp.tile(jnp.arange(ts), bs))

    cache_kT = jnp.transpose(cache_k, (0, 1, 3, 4, 2))
    state_tm = jnp.transpose(state_pool, (0, 2, 1, 3))
    cache_v2 = cache_v.reshape(cache_v.shape[0], cache_v.shape[1], PAGE_SIZE * n_heads, vd)
    pt_flat = page_table.reshape(-1)
    bounded = _scores_bounded(q_norm, k_norm)

    xp = x_prompt
    xs = x_sample.reshape(bs * ts, d)
    nk_p, nv_p, nk_s, nv_s, np_p, np_s = [], [], [], [], [], []
    for i in range(depth):
        j = i // N_MIXERS
        if i % N_MIXERS == 0:
            xp, pool_p = _pool_prompt(xp, norm_mix, w_pool16, pool_scale, i, j, tm=512)
            np_p.append(pool_p[:, POOL_HALO - POOL_BUF:])
            xs_tm = jnp.swapaxes(xs.reshape(bs, ts, d), 0, 1)
            xs_tm, pool_s = _pool_sample(xs_tm, state_tm, norm_mix, w_pool16, pool_scale, i, j,
                                         pos0=past_len, bb=32)
            xs = jnp.swapaxes(xs_tm, 0, 1).reshape(bs * ts, d)
            np_s.append(pool_s)
        else:
            lam_init = 0.8 - 0.6 * math.exp(-0.3 * i)
            qT, kT32, kT16, v32, v16 = _qkv(xp, norm_mix, wqkT16, wv16, gq, gk, cos_p, sin_p, i, j,
                                            lambda bi, ti: (0, ti), tm=512)
            o_p = _attn_prompt(bounded[j:j + 1], qT, kT16, v16, lam_params, subln, j, lam_init,
                               tq=1024, tk=512, rc=256)
            xp = _proj_res(xp.reshape(bp * sp, d), o_p.reshape(bp * sp, d), w_o16, j,
                           tm=1024).reshape(bp, sp, d)
            nk_p.append(kT32.reshape(bp, n_hc, HEAD_DIM, sp))
            nv_p.append(v32.reshape(bp, sp, n_heads, vd))
            qT_s, kT32_s, _, v32_s, _ = _qkv(xs[None], norm_mix, wqkT16, wv16, gq, gk, cos_s, sin_s,
                                             i, j, lambda bi, ti: (0, ti), tm=512)
            q_s = qT_s[0].T.reshape(bs, ts, d)
            kn_s = kT32_s[0].T.reshape(bs, ts, d)
            vn_s = v32_s.reshape(bs, ts, d)
            o_s = _attn_decode(pt_flat, q_s, kn_s, vn_s, lam_params, subln, cache_kT, cache_v2, j,
                               lam_init)
            xs = _proj_res(xs, o_s.reshape(bs * ts, d), w_o16, j, tm=1024)
            nk_s.append(kn_s.reshape(bs, ts, n_hc, HEAD_DIM))
            nv_s.append(vn_s.reshape(bs, ts, n_heads, vd))
        xp = _ffn(xp.reshape(bp * sp, d), norm_ffn, w_up16, w_down16, i, tm=1024,
                  tf=1024).reshape(bp, sp, d)
        xs = _ffn(xs, norm_ffn, w_up16, w_down16, i, tm=1024, tf=1024)

    new_k_prompt = jnp.transpose(jnp.stack(nk_p), (0, 1, 4, 2, 3))
    new_v_prompt = jnp.stack(nv_p)
    new_k_sample = jnp.stack(nk_s)
    new_v_sample = jnp.stack(nv_s)
    new_pool_prompt = jnp.stack(np_p)
    new_pool_sample = jnp.transpose(jnp.stack(np_s), (0, 2, 1, 3))
    return (xp, xs.reshape(bs, ts, d), new_k_prompt, new_v_prompt, new_k_sample, new_v_sample,
            new_pool_prompt, new_pool_sample)
```

```python
import functools
import math

import jax
import jax.numpy as jnp
from jax import lax
from jax.experimental import pallas as pl
from jax.experimental.pallas import tpu as pltpu

F32 = jnp.float32
BF16 = jnp.bfloat16

N_MIXERS = 2
POOL_WINDOWS = (2, 4, 8, 16)
POOL_BUF = max(POOL_WINDOWS) - 1
POOL_HALO = 16
HEAD_DIM = 64
HALF = HEAD_DIM // 2
ROPE_THETA = 10000.0
EPS = 1e-6
SUBLN_EPS = 1e-5
PAGE_SIZE = 128
SCORE_BOUND_LOG2 = 48.0
Q_SCALE = math.log2(math.e) * HEAD_DIM ** -0.5
FFN_ROWS = 1024
FFN_COLS = 1024

V7X_VMEM_LIMIT_BYTES = 56 * 1024 * 1024


def _cparams(*sem):
    return pltpu.CompilerParams(dimension_semantics=sem, vmem_limit_bytes=V7X_VMEM_LIMIT_BYTES)


def _rms(x, g, eps=EPS):
    ms = jnp.mean(x * x, axis=-1, keepdims=True)
    return x * lax.rsqrt(ms + eps) * g


def _mlp_chunk(h_scr, wu_ref, wd_ref, o_ref):
    u = jnp.dot(h_scr[...], wu_ref[...], preferred_element_type=F32)
    a = jnp.maximum(u, 0.0)
    a = (a * a).astype(BF16)
    o_ref[...] += jnp.dot(a, wd_ref[...], preferred_element_type=F32)


def _ffn_kernel(x_ref, g_ref, wu_ref, wd_ref, o_ref, h_scr):
    @pl.when(pl.program_id(1) == 0)
    def _():
        x = x_ref[...]
        h_scr[...] = _rms(x, g_ref[...]).astype(BF16)
        o_ref[...] = x

    _mlp_chunk(h_scr, wu_ref, wd_ref, o_ref)


def _proj_ffn_kernel(x_ref, a_ref, wo_ref, g_ref, wu_ref, wd_ref, o_ref, h_scr):
    @pl.when(pl.program_id(1) == 0)
    def _():
        x = x_ref[...] + jnp.dot(a_ref[...], wo_ref[...], preferred_element_type=F32)
        h_scr[...] = _rms(x, g_ref[...]).astype(BF16)
        o_ref[...] = x

    _mlp_chunk(h_scr, wu_ref, wd_ref, o_ref)


def _pool_ffn_kernel(x_ref, prev_ref, gm_ref, wp_ref, sc_ref, g_ref, wu_ref, wd_ref, o_ref, np_ref,
                     h_scr, ext_scr, *, tm, tiles_per_seq):
    @pl.when(pl.program_id(1) == 0)
    def _():
        ti = pl.program_id(0) % tiles_per_seq
        gm = gm_ref[...]
        x = x_ref[...]
        h = _rms(x, gm)
        hp = _rms(prev_ref[...], gm)
        hp = jnp.where(ti == 0, 0.0, hp)
        ext_scr[0:POOL_HALO, :] = hp
        ext_scr[POOL_HALO:POOL_HALO + tm, :] = h
        pos1 = ti * tm + lax.broadcasted_iota(jnp.int32, (tm, 1), 0) + 1
        gw = x.shape[1] // len(POOL_WINDOWS)
        for gi, w in enumerate(POOL_WINDOWS):
            lo = gi * gw
            hg = h[:, lo:lo + gw]
            s = hg
            for j in range(1, w):
                s = s + ext_scr[POOL_HALO - j:POOL_HALO - j + tm, lo:lo + gw]
            cnt = jnp.minimum(pos1, w).astype(F32)
            dlt = (s / cnt - hg).astype(BF16)
            y = jnp.dot(dlt, wp_ref[gi], preferred_element_type=F32)
            o_ref[:, lo:lo + gw] = x[:, lo:lo + gw] + y * sc_ref[:, lo:lo + gw]
        h_scr[...] = _rms(o_ref[...], g_ref[...]).astype(BF16)

        @pl.when(ti == tiles_per_seq - 1)
        def _():
            np_ref[...] = ext_scr[tm:tm + POOL_HALO, :]

    _mlp_chunk(h_scr, wu_ref, wd_ref, o_ref)


def _ffn(x2d, g, w_up, w_down, layer, *, tm, tf, proj=None, pool=None):
    n, d = x2d.shape
    dff = w_up.shape[-1]
    row = pl.BlockSpec((tm, d), lambda i, c: (i, 0))
    mlp_specs = [
        pl.BlockSpec((None, 1, d), lambda i, c: (layer, 0, 0)),
        pl.BlockSpec((None, d, tf), lambda i, c: (layer, 0, c)),
        pl.BlockSpec((None, tf, d), lambda i, c: (layer, c, 0)),
    ]
    mlp_args = [g, w_up, w_down]
    out_specs, out_shape = row, jax.ShapeDtypeStruct((n, d), F32)
    scratch = [pltpu.VMEM((tm, d), BF16)]
    if proj is not None:
        a2d, w_o, j = proj
        body, name = _proj_ffn_kernel, "proj_ffn"
        in_specs = [row, row, pl.BlockSpec((None, d, d), lambda i, c: (j, 0, 0))] + mlp_specs
        args = [x2d, a2d, w_o] + mlp_args
    elif pool is not None:
        gm, w_pool, scale, j, seq_len = pool
        tiles_per_seq = seq_len // tm
        halo_blocks = tm // POOL_HALO
        n_groups = len(POOL_WINDOWS)
        gw = d // n_groups
        body = functools.partial(_pool_ffn_kernel, tm=tm, tiles_per_seq=tiles_per_seq)
        name = "pool_ffn"
        in_specs = [
            row,
            pl.BlockSpec((POOL_HALO, d), lambda i, c: (jnp.maximum(i * halo_blocks - 1, 0), 0)),
            pl.BlockSpec((None, 1, d), lambda i, c: (layer, 0, 0)),
            pl.BlockSpec((None, n_groups, gw, gw), lambda i, c: (j, 0, 0, 0)),
            pl.BlockSpec((None, 1, d), lambda i, c: (j, 0, 0)),
        ] + mlp_specs
        args = [x2d, x2d, gm, w_pool, scale] + mlp_args
        out_specs = [row, pl.BlockSpec((POOL_HALO, d), lambda i, c: (i // tiles_per_seq, 0))]
        out_shape = [out_shape,
                     jax.ShapeDtypeStruct((n // seq_len * POOL_HALO, d), F32)]
        scratch = scratch + [pltpu.VMEM((POOL_HALO + tm, d), F32)]
    else:
        body, name = _ffn_kernel, "ffn"
        in_specs = [row] + mlp_specs
        args = [x2d] + mlp_args
    return pl.pallas_call(
        body,
        grid=(n // tm, dff // tf),
        in_specs=in_specs,
        out_specs=out_specs,
        out_shape=out_shape,
        scratch_shapes=scratch,
        compiler_params=_cparams("parallel", "arbitrary"),
        name=name,
    )(*args)


def _pool_sample_kernel(x_ref, st_ref, g_ref, wp_ref, sc_ref, o_ref, np_ref, *, pos0):
    ts = x_ref.shape[0]
    g = g_ref[...]
    xs = [x_ref[t] for t in range(ts)]
    ext = [st_ref[r] for r in range(POOL_BUF)] + [_rms(xt, g) for xt in xs]
    gw = xs[0].shape[1] // len(POOL_WINDOWS)
    for t in range(ts):
        e = POOL_BUF + t
        for gi, w in enumerate(POOL_WINDOWS):
            lo = gi * gw
            s = ext[e][:, lo:lo + gw]
            for jj in range(1, w):
                s = s + ext[e - jj][:, lo:lo + gw]
            cnt = float(min(pos0 + t + 1, w))
            dlt = (s / cnt - ext[e][:, lo:lo + gw]).astype(BF16)
            y = jnp.dot(dlt, wp_ref[gi], preferred_element_type=F32)
            o_ref[t, :, lo:lo + gw] = xs[t][:, lo:lo + gw] + y * sc_ref[:, lo:lo + gw]
    for r in range(POOL_BUF):
        np_ref[r] = ext[ts + r]


def _pool_sample(x_tm, state_tm, g_all, w_pool, scale_all, layer, j, *, pos0, bb):
    ts, b, d = x_tm.shape
    n_groups = len(POOL_WINDOWS)
    gw = d // n_groups
    return pl.pallas_call(
        functools.partial(_pool_sample_kernel, pos0=pos0),
        grid=(b // bb,),
        in_specs=[
            pl.BlockSpec((ts, bb, d), lambda i: (0, i, 0)),
            pl.BlockSpec((None, POOL_BUF, bb, d), lambda i: (j, 0, i, 0)),
            pl.BlockSpec((None, 1, d), lambda i: (layer, 0, 0)),
            pl.BlockSpec((None, n_groups, gw, gw), lambda i: (j, 0, 0, 0)),
            pl.BlockSpec((None, 1, d), lambda i: (j, 0, 0)),
        ],
        out_specs=[
            pl.BlockSpec((ts, bb, d), lambda i: (0, i, 0)),
            pl.BlockSpec((POOL_BUF, bb, d), lambda i: (0, i, 0)),
        ],
        out_shape=[
            jax.ShapeDtypeStruct((ts, b, d), F32),
            jax.ShapeDtypeStruct((POOL_BUF, b, d), F32),
        ],
        compiler_params=_cparams("parallel"),
        name="pool_sample",
    )(x_tm, state_tm, g_all, w_pool, scale_all)


def _qkv_kernel(x_ref, g_ref, wqk_ref, wv_ref, gq_ref, gk_ref, cos_ref, sin_ref, *rest, q_scale):
    qT_ref, kT32_ref, kT16_ref, v32_ref, v16_ref = rest[-5:]
    x = x_ref[0]
    tm, d = x.shape
    h = _rms(x, g_ref[...]).astype(BF16)
    v = jnp.dot(h, wv_ref[...], preferred_element_type=F32)
    v32_ref[0] = v
    v16_ref[0] = v.astype(BF16)
    cos = cos_ref[...][None]
    sin = sin_ref[...][None]
    rows = 4 * HEAD_DIM
    for part in range(2):
        gn = (gq_ref if part == 0 else gk_ref)[...][None]
        for c in range(d // rows):
            r0 = part * d + c * rows
            yT = lax.dot_general(wqk_ref[r0:r0 + rows, :], h, (((1,), (1,)), ((), ())),
                                 preferred_element_type=F32)
            y3 = yT.reshape(rows // HEAD_DIM, HEAD_DIM, tm)
            ms = jnp.mean(y3 * y3, axis=1, keepdims=True)
            yn = y3 * lax.rsqrt(ms + EPS) * gn
            x1 = yn[:, :HALF, :]
            x2 = yn[:, HALF:, :]
            out = jnp.concatenate([x1 * cos - x2 * sin, x2 * cos + x1 * sin], axis=1)
            out = out.reshape(rows, tm)
            if part == 0:
                qT_ref[0, c * rows:(c + 1) * rows, :] = out * q_scale
            else:
                kT32_ref[0, c * rows:(c + 1) * rows, :] = out
                kT16_ref[0, c * rows:(c + 1) * rows, :] = out.astype(BF16)


def _qkv(x, g_all, wqkT, wv, gq, gk, cosT, sinT, layer, j, tab_map, *, tm, n_slots=1, slot=0,
         prev=None):
    b, t, d = x.shape
    tok = lambda bi, i: (bi, i, 0)
    tr = lambda bi, i: (bi, 0, i)
    in_specs = [
        pl.BlockSpec((1, tm, d), tok),
        pl.BlockSpec((None, 1, d), lambda bi, i: (layer, 0, 0)),
        pl.BlockSpec((None, 2 * d, d), lambda bi, i: (j, 0, 0)),
        pl.BlockSpec((None, d, d), lambda bi, i: (j, 0, 0)),
        pl.BlockSpec((None, HEAD_DIM, 1), lambda bi, i: (j, 0, 0)),
        pl.BlockSpec((None, HEAD_DIM, 1), lambda bi, i: (j, 0, 0)),
        pl.BlockSpec((HALF, tm), tab_map),
        pl.BlockSpec((HALF, tm), tab_map),
    ]
    args = [x, g_all, wqkT, wv, gq, gk, cosT, sinT]
    aliases = {}
    if prev is not None:
        aliases = {len(args): 1, len(args) + 1: 3}
        in_specs = in_specs + [pl.BlockSpec(memory_space=pl.ANY)] * 2
        args = args + list(prev)
    return pl.pallas_call(
        functools.partial(_qkv_kernel, q_scale=Q_SCALE),
        grid=(b, t // tm),
        in_specs=in_specs,
        out_specs=[
            pl.BlockSpec((1, d, tm), tr),
            pl.BlockSpec((None, 1, d, tm), lambda bi, i: (slot, bi, 0, i)),
            pl.BlockSpec((1, d, tm), tr),
            pl.BlockSpec((None, 1, tm, d), lambda bi, i: (slot, bi, i, 0)),
            pl.BlockSpec((1, tm, d), tok),
        ],
        out_shape=[
            jax.ShapeDtypeStruct((b, d, t), F32),
            jax.ShapeDtypeStruct((n_slots, b, d, t), F32),
            jax.ShapeDtypeStruct((b, d, t), BF16),
            jax.ShapeDtypeStruct((n_slots, b, t, d), F32),
            jax.ShapeDtypeStruct((b, t, d), BF16),
        ],
        input_output_aliases=aliases,
        compiler_params=_cparams("parallel", "parallel"),
        name="qkv",
    )(*args)


def _lambda_full(lam_ref, lam_init):
    lp = lam_ref[...]
    a = jnp.sum(lp[0:1] * lp[1:2], axis=-1, keepdims=True)
    b = jnp.sum(lp[2:3] * lp[3:4], axis=-1, keepdims=True)
    return jnp.exp(a) - jnp.exp(b) + lam_init


def _diff_combine(o1, l1, o2, l2, lam, subln, lam_init):
    o = o1 / l1 - lam * (o2 / l2)
    return _rms(o, subln, SUBLN_EPS) * (1.0 - lam_init)


def _attn_kernel(bounded_ref, qT_ref, kT_ref, v_ref, lam_ref, subln_ref, o_ref, qs_scr, m_scr,
                 acc_scr, *, tq, tk, rc, lam_init):
    qi = pl.program_id(2)
    vd = v_ref.shape[2]
    q = qT_ref[0].T
    lane = lax.broadcasted_iota(jnp.int32, q.shape, 1)
    qs_scr[0:tq, :] = jnp.where(lane < HEAD_DIM, q, 0.0).astype(BF16)
    qs_scr[tq:2 * tq, :] = jnp.where(lane >= HEAD_DIM, q, 0.0).astype(BF16)
    acc_scr[...] = jnp.zeros(acc_scr.shape, F32)
    ones = jnp.ones((tk, vd), BF16)
    lower_tri = (lax.broadcasted_iota(jnp.int32, (rc, rc), 1)
                 <= lax.broadcasted_iota(jnp.int32, (rc, rc), 0))

    def keys(k0, n):
        kT = kT_ref[0, :, pl.ds(k0, n)]
        vv = jnp.concatenate([v_ref[0, pl.ds(k0, n), :], ones[:n]], axis=1)
        return kT, vv

    def update(r, kT, vv, bounded, triangle=False):
        s = jnp.dot(qs_scr[r:r + rc, :], kT, preferred_element_type=F32)
        if triangle:
            s = jnp.where(lower_tri, s, -jnp.inf)
        if bounded:
            pv = jnp.dot(jnp.exp2(s).astype(BF16), vv, preferred_element_type=F32)
            acc_scr[r:r + rc, :] += pv
        else:
            m_old = m_scr[r:r + rc, :]
            m_new = jnp.maximum(m_old, jnp.max(s, axis=1, keepdims=True))
            alpha = jnp.exp2(m_old - m_new)
            p = jnp.exp2(s - m_new)
            pv = jnp.dot(p.astype(BF16), vv, preferred_element_type=F32)
            acc_scr[r:r + rc, :] = acc_scr[r:r + rc, :] * alpha + pv
            m_scr[r:r + rc, :] = m_new

    def sweep(bounded):
        def body(jb, carry):
            kT, vv = keys(pl.multiple_of(jb * tk, tk), tk)
            for r in range(0, 2 * tq, rc):
                update(r, kT, vv, bounded)
            return carry

        lax.fori_loop(0, qi * (tq // tk), body, 0)
        base = pl.multiple_of(qi * tq, tq)
        for r in range(0, 2 * tq, rc):
            off = r % tq
            for c0 in range(0, off, tk):
                kT, vv = keys(pl.multiple_of(base + c0, rc), min(tk, off - c0))
                update(r, kT, vv, bounded)
            kT, vv = keys(pl.multiple_of(base + off, rc), rc)
            update(r, kT, vv, bounded, triangle=True)

    is_bounded = bounded_ref[0] == 1

    @pl.when(is_bounded)
    def _():
        sweep(True)

    @pl.when(jnp.logical_not(is_bounded))
    def _():
        m_scr[...] = jnp.full(m_scr.shape, -jnp.inf, F32)
        sweep(False)

    lam = _lambda_full(lam_ref, lam_init)
    acc = acc_scr[...]
    o = _diff_combine(acc[0:tq, 0:vd], acc[0:tq, vd:vd + 1], acc[tq:, 0:vd], acc[tq:, vd:vd + 1],
                      lam, subln_ref[...], lam_init)
    o_ref[0] = o.astype(BF16)


def _attn_prompt(bounded, qT, kT16, v16, lam_params, subln_all, j, lam_init, *, tq, tk, rc):
    b, d, t = qT.shape
    vd = 2 * HEAD_DIM
    n_heads = d // vd
    grid_spec = pltpu.PrefetchScalarGridSpec(
        num_scalar_prefetch=1,
        grid=(b, n_heads, t // tq),
        in_specs=[
            pl.BlockSpec((1, vd, tq), lambda bi, hi, qi, fl: (bi, hi, qi)),
            pl.BlockSpec((1, vd, t), lambda bi, hi, qi, fl: (bi, hi, 0)),
            pl.BlockSpec((1, t, vd), lambda bi, hi, qi, fl: (bi, 0, hi)),
            pl.BlockSpec((None, 4, HEAD_DIM), lambda bi, hi, qi, fl: (j, 0, 0)),
            pl.BlockSpec((None, 1, vd), lambda bi, hi, qi, fl: (j, 0, 0)),
        ],
        out_specs=pl.BlockSpec((1, tq, vd), lambda bi, hi, qi, fl: (bi, qi, hi)),
        scratch_shapes=[
            pltpu.VMEM((2 * tq, vd), BF16),
            pltpu.VMEM((2 * tq, 1), F32),
            pltpu.VMEM((2 * tq, 2 * vd), F32),
        ],
    )
    return pl.pallas_call(
        functools.partial(_attn_kernel, tq=tq, tk=tk, rc=rc, lam_init=lam_init),
        grid_spec=grid_spec,
        out_shape=jax.ShapeDtypeStruct((b, t, d), BF16),
        compiler_params=_cparams("parallel", "parallel", "arbitrary"),
        name="attn_prompt",
    )(bounded, qT, kT16, v16, lam_params, subln_all)


def _scores_bounded(q_gain, k_gain):
    bound = (math.sqrt(HEAD_DIM) * math.log2(math.e)
             * jnp.max(jnp.abs(q_gain), axis=-1) * jnp.max(jnp.abs(k_gain), axis=-1))
    return (bound <= SCORE_BOUND_LOG2).astype(jnp.int32)


def _decode_kernel(pt_ref, q_ref, kn_ref, vn_ref, lam_ref, subln_ref, *rest, n_pages, lam_init):
    del pt_ref
    k_refs = rest[:n_pages]
    v_refs = rest[n_pages:2 * n_pages]
    o_ref, s_scr, p_scr = rest[2 * n_pages:]
    q = q_ref[0]
    kn = kn_ref[0]
    vn = vn_ref[0]
    ts = q.shape[0]
    vd = 2 * HEAD_DIM
    n_heads = q.shape[1] // vd
    causal = (lax.broadcasted_iota(jnp.int32, (ts, ts), 1)
              <= lax.broadcasted_iota(jnp.int32, (ts, ts), 0))
    lam = _lambda_full(lam_ref, lam_init)
    subln = subln_ref[...]
    nt = (((1,), (1,)), ((), ()))
    n_hc = 2 * n_heads
    qh = [q[:, hc * HEAD_DIM:(hc + 1) * HEAD_DIM] for hc in range(n_hc)]
    for hc in range(n_hc):
        for p in range(n_pages):
            s_scr[hc, :, p * PAGE_SIZE:(p + 1) * PAGE_SIZE] = jnp.dot(
                qh[hc], k_refs[p][0, 0, hc], preferred_element_type=F32)
    p_new, row_sum = [], []
    for hc in range(n_hc):
        s = s_scr[hc]
        s_new = lax.dot_general(qh[hc], kn[:, hc * HEAD_DIM:(hc + 1) * HEAD_DIM], nt,
                                preferred_element_type=F32)
        s_new = jnp.where(causal, s_new, -jnp.inf)
        m = jnp.maximum(jnp.max(s, axis=1, keepdims=True), jnp.max(s_new, axis=1, keepdims=True))
        pr = jnp.exp2(s - m)
        pn = jnp.exp2(s_new - m)
        p_scr[hc // 2, (hc % 2) * ts:(hc % 2 + 1) * ts, :] = pr
        p_new.append(pn)
        row_sum.append(jnp.sum(pr, axis=1, keepdims=True) + jnp.sum(pn, axis=1, keepdims=True))
    for h in range(n_heads):
        acc = jnp.dot(jnp.concatenate([p_new[2 * h], p_new[2 * h + 1]], axis=0),
                      vn[:, h * vd:(h + 1) * vd], preferred_element_type=F32)
        for p in range(n_pages):
            vh = v_refs[p][pl.ds(h, PAGE_SIZE, stride=n_heads), :]
            acc = acc + jnp.dot(p_scr[h, :, p * PAGE_SIZE:(p + 1) * PAGE_SIZE], vh,
                                preferred_element_type=F32)
        o = _diff_combine(acc[:ts], row_sum[2 * h], acc[ts:], row_sum[2 * h + 1], lam, subln,
                          lam_init)
        o_ref[0, :, h * vd:(h + 1) * vd] = o.astype(BF16)


def _attn_decode(page_table_flat, q, kn, vn, lam_params, subln_all, cache_kT, cache_v, j, lam_init):
    b, ts, d = q.shape
    vd = 2 * HEAD_DIM
    n_pages = page_table_flat.shape[0] // b
    n_hc = cache_kT.shape[2]
    row = lambda bi, pt: (bi, 0, 0)

    def k_map(p):
        return lambda bi, pt: (j, pt[bi * n_pages + p], 0, 0, 0)

    def v_map(p):
        return lambda bi, pt: (j, pt[bi * n_pages + p], 0, 0)

    k_specs = [pl.BlockSpec((1, 1, n_hc, HEAD_DIM, PAGE_SIZE), k_map(p)) for p in range(n_pages)]
    v_specs = [pl.BlockSpec((None, None, cache_v.shape[2], vd), v_map(p)) for p in range(n_pages)]
    grid_spec = pltpu.PrefetchScalarGridSpec(
        num_scalar_prefetch=1,
        grid=(b,),
        in_specs=[
            pl.BlockSpec((1, ts, d), row),
            pl.BlockSpec((1, ts, d), row),
            pl.BlockSpec((1, ts, d), row),
            pl.BlockSpec((None, 4, HEAD_DIM), lambda bi, pt: (j, 0, 0)),
            pl.BlockSpec((None, 1, vd), lambda bi, pt: (j, 0, 0)),
        ] + k_specs + v_specs,
        out_specs=pl.BlockSpec((1, ts, d), row),
        scratch_shapes=[
            pltpu.VMEM((n_hc, ts, n_pages * PAGE_SIZE), F32),
            pltpu.VMEM((n_hc // 2, 2 * ts, n_pages * PAGE_SIZE), F32),
        ],
    )
    return pl.pallas_call(
        functools.partial(_decode_kernel, n_pages=n_pages, lam_init=lam_init),
        grid_spec=grid_spec,
        out_shape=jax.ShapeDtypeStruct((b, ts, d), BF16),
        compiler_params=_cparams("arbitrary"),
        name="attn_decode",
    )(page_table_flat, q, kn, vn, lam_params, subln_all, *([cache_kT] * n_pages), *([cache_v] * n_pages))


def _rope_tables(pos):
    inv = ROPE_THETA ** (-jnp.arange(HALF, dtype=F32) / HALF)
    ang = inv[:, None] * pos.astype(F32)[None, :]
    return jnp.cos(ang), jnp.sin(ang)


def kernel(x_prompt, x_sample, cache_k, cache_v, page_table, state_pool, norm_mix, norm_ffn, w_pool,
           pool_scale, w_qkv, q_norm, k_norm, lambda_q1, lambda_k1, lambda_q2, lambda_k2, subln, w_o,
           w_up, w_down):
    bp, sp, d = x_prompt.shape
    bs, ts, _ = x_sample.shape
    depth = norm_mix.shape[0]
    past_len = page_table.shape[1] * PAGE_SIZE
    n_hc = d // HEAD_DIM
    n_heads = n_hc // 2
    vd = 2 * HEAD_DIM

    w_up16 = w_up.astype(BF16)
    w_down16 = w_down.astype(BF16)
    w_pool16 = w_pool.astype(BF16)
    w_o16 = w_o.astype(BF16)
    wqkT16 = jnp.swapaxes(w_qkv[:, :, :2 * d], 1, 2).astype(BF16)
    wv16 = w_qkv[:, :, 2 * d:].astype(BF16)
    gq = q_norm[:, :, None]
    gk = k_norm[:, :, None]
    lam_params = jnp.stack([lambda_q1, lambda_k1, lambda_q2, lambda_k2], axis=1)
    norm_mix = norm_mix[:, None, :]
    norm_ffn = norm_ffn[:, None, :]
    pool_scale = pool_scale[:, None, :]
    subln = subln[:, None, :]

    cos_p, sin_p = _rope_tables(jnp.arange(sp))
    cos_s, sin_s = _rope_tables(past_len + jnp.tile(jnp.arange(ts), bs))

    cache_kT = jnp.transpose(cache_k, (0, 1, 3, 4, 2))
    state_tm = jnp.transpose(state_pool, (0, 2, 1, 3))
    cache_v2 = cache_v.reshape(cache_v.shape[0], cache_v.shape[1], PAGE_SIZE * n_heads, vd)
    pt_flat = page_table.reshape(-1)
    bounded = _scores_bounded(q_norm, k_norm)

    n_attn = depth // N_MIXERS
    mlp = functools.partial(_ffn, tm=FFN_ROWS, tf=FFN_COLS)
    xp = x_prompt.reshape(bp * sp, d)
    xs = x_sample.reshape(bs * ts, d)
    kv_p = None
    nk_s, nv_s, np_p, np_s = [], [], [], []
    for i in range(depth):
        j = i // N_MIXERS
        if i % N_MIXERS == 0:
            xp, pool_p = mlp(xp, norm_ffn, w_up16, w_down16, i,
                             pool=(norm_mix, w_pool16, pool_scale, j, sp))
            np_p.append(pool_p.reshape(bp, POOL_HALO, d)[:, POOL_HALO - POOL_BUF:])
            xs_tm = jnp.swapaxes(xs.reshape(bs, ts, d), 0, 1)
            xs_tm, pool_s = _pool_sample(xs_tm, state_tm, norm_mix, w_pool16, pool_scale, i, j,
                                         pos0=past_len, bb=32)
            xs = jnp.swapaxes(xs_tm, 0, 1).reshape(bs * ts, d)
            np_s.append(pool_s)
            xs = mlp(xs, norm_ffn, w_up16, w_down16, i)
        else:
            lam_init = 0.8 - 0.6 * math.exp(-0.3 * i)
            qT, kT32, kT16, v32, v16 = _qkv(xp.reshape(bp, sp, d), norm_mix, wqkT16, wv16, gq, gk,
                                            cos_p, sin_p, i, j, lambda bi, ti: (0, ti), tm=512,
                                            n_slots=n_attn, slot=j, prev=kv_p)
            kv_p = (kT32, v32)
            o_p = _attn_prompt(bounded[j:j + 1], qT, kT16, v16, lam_params, subln, j, lam_init,
                               tq=1024, tk=1024, rc=256)
            xp = mlp(xp, norm_ffn, w_up16, w_down16, i, proj=(o_p.reshape(bp * sp, d), w_o16, j))
            qT_s, kT32_s, _, v32_s, _ = _qkv(xs[None], norm_mix, wqkT16, wv16, gq, gk, cos_s, sin_s,
                                             i, j, lambda bi, ti: (0, ti), tm=512)
            q_s = qT_s[0].T.reshape(bs, ts, d)
            kn_s = kT32_s[0, 0].T.reshape(bs, ts, d)
            vn_s = v32_s.reshape(bs, ts, d)
            o_s = _attn_decode(pt_flat, q_s, kn_s, vn_s, lam_params, subln, cache_kT, cache_v2, j,
                               lam_init)
            xs = mlp(xs, norm_ffn, w_up16, w_down16, i, proj=(o_s.reshape(bs * ts, d), w_o16, j))
            nk_s.append(kn_s.reshape(bs, ts, n_hc, HEAD_DIM))
            nv_s.append(vn_s.reshape(bs, ts, n_heads, vd))

    kT32, v32 = kv_p
    new_k_prompt = jnp.transpose(kT32.reshape(n_attn, bp, n_hc, HEAD_DIM, sp), (0, 1, 4, 2, 3))
    new_v_prompt = v32.reshape(n_attn, bp, sp, n_heads, vd)
    new_k_sample = jnp.stack(nk_s)
    new_v_sample = jnp.stack(nv_s)
    new_pool_prompt = jnp.stack(np_p)
    new_pool_sample = jnp.transpose(jnp.stack(np_s), (0, 2, 1, 3))
    return (xp.reshape(bp, sp, d), xs.reshape(bs, ts, d), new_k_prompt, new_v_prompt, new_k_sample,
            new_v_sample, new_pool_prompt, new_pool_sample)
```

```python
import functools
import math

import jax
import jax.numpy as jnp
from jax import lax
from jax.experimental import pallas as pl
from jax.experimental.pallas import tpu as pltpu

F32 = jnp.float32
BF16 = jnp.bfloat16

N_MIXERS = 2
POOL_WINDOWS = (2, 4, 8, 16)
POOL_BUF = max(POOL_WINDOWS) - 1
POOL_HALO = 16
HEAD_DIM = 64
HALF = HEAD_DIM // 2
ROPE_THETA = 10000.0
EPS = 1e-6
SUBLN_EPS = 1e-5
PAGE_SIZE = 128
SCORE_BOUND_LOG2 = 48.0
Q_SCALE = math.log2(math.e) * HEAD_DIM ** -0.5
FFN_ROWS = 1024
FFN_COLS = 1024

V7X_VMEM_LIMIT_BYTES = 56 * 1024 * 1024


def _cparams(*sem):
    return pltpu.CompilerParams(dimension_semantics=sem, vmem_limit_bytes=V7X_VMEM_LIMIT_BYTES)


def _rms(x, g, eps=EPS):
    ms = jnp.mean(x * x, axis=-1, keepdims=True)
    return x * lax.rsqrt(ms + eps) * g


def _mlp_chunk(h_scr, wu_ref, wd_ref, o_ref):
    u = jnp.dot(h_scr[...], wu_ref[...], preferred_element_type=F32)
    a = jnp.maximum(u, 0.0)
    a = (a * a).astype(BF16)
    o_ref[...] += jnp.dot(a, wd_ref[...], preferred_element_type=F32)


def _ffn_kernel(x_ref, g_ref, wu_ref, wd_ref, o_ref, h_scr):
    @pl.when(pl.program_id(1) == 0)
    def _():
        x = x_ref[...]
        h_scr[...] = _rms(x, g_ref[...]).astype(BF16)
        o_ref[...] = x

    _mlp_chunk(h_scr, wu_ref, wd_ref, o_ref)


def _proj_ffn_kernel(x_ref, a_ref, wo_ref, g_ref, wu_ref, wd_ref, o_ref, h_scr):
    @pl.when(pl.program_id(1) == 0)
    def _():
        x = x_ref[...] + jnp.dot(a_ref[...], wo_ref[...], preferred_element_type=F32)
        h_scr[...] = _rms(x, g_ref[...]).astype(BF16)
        o_ref[...] = x

    _mlp_chunk(h_scr, wu_ref, wd_ref, o_ref)


def _pool_ffn_kernel(x_ref, prev_ref, gm_ref, wp_ref, sc_ref, g_ref, wu_ref, wd_ref, o_ref, np_ref,
                     h_scr, ext_scr, *, tm, tiles_per_seq):
    @pl.when(pl.program_id(1) == 0)
    def _():
        ti = pl.program_id(0) % tiles_per_seq
        gm = gm_ref[...]
        x = x_ref[...]
        h = _rms(x, gm)
        hp = _rms(prev_ref[...], gm)
        hp = jnp.where(ti == 0, 0.0, hp)
        ext_scr[0:POOL_HALO, :] = hp
        ext_scr[POOL_HALO:POOL_HALO + tm, :] = h
        pos1 = ti * tm + lax.broadcasted_iota(jnp.int32, (tm, 1), 0) + 1
        gw = x.shape[1] // len(POOL_WINDOWS)
        for gi, w in enumerate(POOL_WINDOWS):
            lo = gi * gw
            hg = h[:, lo:lo + gw]
            s = hg
            for j in range(1, w):
                s = s + ext_scr[POOL_HALO - j:POOL_HALO - j + tm, lo:lo + gw]
            cnt = jnp.minimum(pos1, w).astype(F32)
            dlt = (s / cnt - hg).astype(BF16)
            y = jnp.dot(dlt, wp_ref[gi], preferred_element_type=F32)
            o_ref[:, lo:lo + gw] = x[:, lo:lo + gw] + y * sc_ref[:, lo:lo + gw]
        h_scr[...] = _rms(o_ref[...], g_ref[...]).astype(BF16)

        @pl.when(ti == tiles_per_seq - 1)
        def _():
            np_ref[...] = ext_scr[tm:tm + POOL_HALO, :]

    _mlp_chunk(h_scr, wu_ref, wd_ref, o_ref)


def _ffn(x2d, g, w_up, w_down, layer, *, tm, tf, proj=None, pool=None):
    n, d = x2d.shape
    dff = w_up.shape[-1]
    row = pl.BlockSpec((tm, d), lambda i, c: (i, 0))
    mlp_specs = [
        pl.BlockSpec((None, 1, d), lambda i, c: (layer, 0, 0)),
        pl.BlockSpec((None, d, tf), lambda i, c: (layer, 0, c)),
        pl.BlockSpec((None, tf, d), lambda i, c: (layer, c, 0)),
    ]
    mlp_args = [g, w_up, w_down]
    out_specs, out_shape = row, jax.ShapeDtypeStruct((n, d), F32)
    scratch = [pltpu.VMEM((tm, d), BF16)]
    if proj is not None:
        a2d, w_o, j = proj
        body, name = _proj_ffn_kernel, "proj_ffn"
        in_specs = [row, row, pl.BlockSpec((None, d, d), lambda i, c: (j, 0, 0))] + mlp_specs
        args = [x2d, a2d, w_o] + mlp_args
    elif pool is not None:
        gm, w_pool, scale, j, seq_len = pool
        tiles_per_seq = seq_len // tm
        halo_blocks = tm // POOL_HALO
        n_groups = len(POOL_WINDOWS)
        gw = d // n_groups
        body = functools.partial(_pool_ffn_kernel, tm=tm, tiles_per_seq=tiles_per_seq)
        name = "pool_ffn"
        in_specs = [
            row,
            pl.BlockSpec((POOL_HALO, d), lambda i, c: (jnp.maximum(i * halo_blocks - 1, 0), 0)),
            pl.BlockSpec((None, 1, d), lambda i, c: (layer, 0, 0)),
            pl.BlockSpec((None, n_groups, gw, gw), lambda i, c: (j, 0, 0, 0)),
            pl.BlockSpec((None, 1, d), lambda i, c: (j, 0, 0)),
        ] + mlp_specs
        args = [x2d, x2d, gm, w_pool, scale] + mlp_args
        out_specs = [row, pl.BlockSpec((POOL_HALO, d), lambda i, c: (i // tiles_per_seq, 0))]
        out_shape = [out_shape,
                     jax.ShapeDtypeStruct((n // seq_len * POOL_HALO, d), F32)]
        scratch = scratch + [pltpu.VMEM((POOL_HALO + tm, d), F32)]
    else:
        body, name = _ffn_kernel, "ffn"
        in_specs = [row] + mlp_specs
        args = [x2d] + mlp_args
    return pl.pallas_call(
        body,
        grid=(n // tm, dff // tf),
        in_specs=in_specs,
        out_specs=out_specs,
        out_shape=out_shape,
        scratch_shapes=scratch,
        compiler_params=_cparams("parallel", "arbitrary"),
        name=name,
    )(*args)


def _pool_sample_kernel(x_ref, st_ref, g_ref, wp_ref, sc_ref, o_ref, np_ref, *, pos0):
    ts = x_ref.shape[0]
    g = g_ref[...]
    xs = [x_ref[t] for t in range(ts)]
    ext = [st_ref[r] for r in range(POOL_BUF)] + [_rms(xt, g) for xt in xs]
    gw = xs[0].shape[1] // len(POOL_WINDOWS)
    for t in range(ts):
        e = POOL_BUF + t
        for gi, w in enumerate(POOL_WINDOWS):
            lo = gi * gw
            s = ext[e][:, lo:lo + gw]
            for jj in range(1, w):
                s = s + ext[e - jj][:, lo:lo + gw]
            cnt = float(min(pos0 + t + 1, w))
            dlt = (s / cnt - ext[e][:, lo:lo + gw]).astype(BF16)
            y = jnp.dot(dlt, wp_ref[gi], preferred_element_type=F32)
            o_ref[t, :, lo:lo + gw] = xs[t][:, lo:lo + gw] + y * sc_ref[:, lo:lo + gw]
    for r in range(POOL_BUF):
        np_ref[r] = ext[ts + r]


def _pool_sample(x_tm, state_tm, g_all, w_pool, scale_all, layer, j, *, pos0, bb):
    ts, b, d = x_tm.shape
    n_groups = len(POOL_WINDOWS)
    gw = d // n_groups
    return pl.pallas_call(
        functools.partial(_pool_sample_kernel, pos0=pos0),
        grid=(b // bb,),
        in_specs=[
            pl.BlockSpec((ts, bb, d), lambda i: (0, i, 0)),
            pl.BlockSpec((None, POOL_BUF, bb, d), lambda i: (j, 0, i, 0)),
            pl.BlockSpec((None, 1, d), lambda i: (layer, 0, 0)),
            pl.BlockSpec((None, n_groups, gw, gw), lambda i: (j, 0, 0, 0)),
            pl.BlockSpec((None, 1, d), lambda i: (j, 0, 0)),
        ],
        out_specs=[
            pl.BlockSpec((ts, bb, d), lambda i: (0, i, 0)),
            pl.BlockSpec((POOL_BUF, bb, d), lambda i: (0, i, 0)),
        ],
        out_shape=[
            jax.ShapeDtypeStruct((ts, b, d), F32),
            jax.ShapeDtypeStruct((POOL_BUF, b, d), F32),
        ],
        compiler_params=_cparams("parallel"),
        name="pool_sample",
    )(x_tm, state_tm, g_all, w_pool, scale_all)


def _qkv_kernel(x_ref, g_ref, wqk_ref, wv_ref, gq_ref, gk_ref, cos_ref, sin_ref, *rest, q_scale):
    qT_ref, kT32_ref, kT16_ref, v32_ref, v16_ref = rest[-5:]
    x = x_ref[0]
    tm, d = x.shape
    h = _rms(x, g_ref[...]).astype(BF16)
    v = jnp.dot(h, wv_ref[...], preferred_element_type=F32)
    v32_ref[0] = v
    v16_ref[0] = v.astype(BF16)
    cos = cos_ref[...][None]
    sin = sin_ref[...][None]
    rows = 4 * HEAD_DIM
    for part in range(2):
        gn = (gq_ref if part == 0 else gk_ref)[...][None]
        for c in range(d // rows):
            r0 = part * d + c * rows
            yT = lax.dot_general(wqk_ref[r0:r0 + rows, :], h, (((1,), (1,)), ((), ())),
                                 preferred_element_type=F32)
            y3 = yT.reshape(rows // HEAD_DIM, HEAD_DIM, tm)
            ms = jnp.mean(y3 * y3, axis=1, keepdims=True)
            yn = y3 * lax.rsqrt(ms + EPS) * gn
            x1 = yn[:, :HALF, :]
            x2 = yn[:, HALF:, :]
            out = jnp.concatenate([x1 * cos - x2 * sin, x2 * cos + x1 * sin], axis=1)
            out = out.reshape(rows, tm)
            if part == 0:
                qT_ref[0, c * rows:(c + 1) * rows, :] = out * q_scale
            else:
                kT32_ref[0, c * rows:(c + 1) * rows, :] = out
                kT16_ref[0, c * rows:(c + 1) * rows, :] = out.astype(BF16)


def _qkv(x, g_all, wqkT, wv, gq, gk, cosT, sinT, layer, j, tab_map, *, tm, n_slots=1, slot=0,
         prev=None):
    b, t, d = x.shape
    tok = lambda bi, i: (bi, i, 0)
    tr = lambda bi, i: (bi, 0, i)
    in_specs = [
        pl.BlockSpec((1, tm, d), tok),
        pl.BlockSpec((None, 1, d), lambda bi, i: (layer, 0, 0)),
        pl.BlockSpec((None, 2 * d, d), lambda bi, i: (j, 0, 0)),
        pl.BlockSpec((None, d, d), lambda bi, i: (j, 0, 0)),
        pl.BlockSpec((None, HEAD_DIM, 1), lambda bi, i: (j, 0, 0)),
        pl.BlockSpec((None, HEAD_DIM, 1), lambda bi, i: (j, 0, 0)),
        pl.BlockSpec((HALF, tm), tab_map),
        pl.BlockSpec((HALF, tm), tab_map),
    ]
    args = [x, g_all, wqkT, wv, gq, gk, cosT, sinT]
    aliases = {}
    if prev is not None:
        aliases = {len(args): 1, len(args) + 1: 3}
        in_specs = in_specs + [pl.BlockSpec(memory_space=pl.ANY)] * 2
        args = args + list(prev)
    return pl.pallas_call(
        functools.partial(_qkv_kernel, q_scale=Q_SCALE),
        grid=(b, t // tm),
        in_specs=in_specs,
        out_specs=[
            pl.BlockSpec((1, d, tm), tr),
            pl.BlockSpec((None, 1, d, tm), lambda bi, i: (slot, bi, 0, i)),
            pl.BlockSpec((1, d, tm), tr),
            pl.BlockSpec((None, 1, tm, d), lambda bi, i: (slot, bi, i, 0)),
            pl.BlockSpec((1, tm, d), tok),
        ],
        out_shape=[
            jax.ShapeDtypeStruct((b, d, t), F32),
            jax.ShapeDtypeStruct((n_slots, b, d, t), F32),
            jax.ShapeDtypeStruct((b, d, t), BF16),
            jax.ShapeDtypeStruct((n_slots, b, t, d), F32),
            jax.ShapeDtypeStruct((b, t, d), BF16),
        ],
        input_output_aliases=aliases,
        compiler_params=_cparams("parallel", "parallel"),
        name="qkv",
    )(*args)


def _lambda_full(lam_ref, lam_init):
    lp = lam_ref[...]
    a = jnp.sum(lp[0:1] * lp[1:2], axis=-1, keepdims=True)
    b = jnp.sum(lp[2:3] * lp[3:4], axis=-1, keepdims=True)
    return jnp.exp(a) - jnp.exp(b) + lam_init


def _diff_combine(o1, l1, o2, l2, lam, subln, lam_init):
    o = o1 / l1 - lam * (o2 / l2)
    return _rms(o, subln, SUBLN_EPS) * (1.0 - lam_init)


def _sample_attention(i, q_ref, kn_ref, vn_ref, k_refs, v_refs, o_ref, s_scr, p_scr, lam, subln,
                      lam_init):
    n_pages = len(k_refs)
    q = q_ref[i]
    kn = kn_ref[i]
    vn = vn_ref[i]
    ts = q.shape[0]
    vd = 2 * HEAD_DIM
    n_heads = q.shape[1] // vd
    n_hc = 2 * n_heads
    causal = (lax.broadcasted_iota(jnp.int32, (ts, ts), 1)
              <= lax.broadcasted_iota(jnp.int32, (ts, ts), 0))
    nt = (((1,), (1,)), ((), ()))
    qh = [q[:, hc * HEAD_DIM:(hc + 1) * HEAD_DIM] for hc in range(n_hc)]
    for hc in range(n_hc):
        for p in range(n_pages):
            s_scr[hc, :, p * PAGE_SIZE:(p + 1) * PAGE_SIZE] = jnp.dot(
                qh[hc], k_refs[p][0, 0, hc], preferred_element_type=F32)
    p_new, row_sum = [], []
    for hc in range(n_hc):
        s = s_scr[hc]
        s_new = lax.dot_general(qh[hc], kn[:, hc * HEAD_DIM:(hc + 1) * HEAD_DIM], nt,
                                preferred_element_type=F32)
        s_new = jnp.where(causal, s_new, -jnp.inf)
        m = jnp.maximum(jnp.max(s, axis=1, keepdims=True), jnp.max(s_new, axis=1, keepdims=True))
        pr = jnp.exp2(s - m)
        pn = jnp.exp2(s_new - m)
        p_scr[hc // 2, (hc % 2) * ts:(hc % 2 + 1) * ts, :] = pr
        p_new.append(pn)
        row_sum.append(jnp.sum(pr, axis=1, keepdims=True) + jnp.sum(pn, axis=1, keepdims=True))
    for h in range(n_heads):
        acc = jnp.dot(jnp.concatenate([p_new[2 * h], p_new[2 * h + 1]], axis=0),
                      vn[:, h * vd:(h + 1) * vd], preferred_element_type=F32)
        for p in range(n_pages):
            vh = v_refs[p][pl.ds(h, PAGE_SIZE, stride=n_heads), :]
            acc = acc + jnp.dot(p_scr[h, :, p * PAGE_SIZE:(p + 1) * PAGE_SIZE], vh,
                                preferred_element_type=F32)
        o = _diff_combine(acc[:ts], row_sum[2 * h], acc[ts:], row_sum[2 * h + 1], lam, subln,
                          lam_init)
        o_ref[i, :, h * vd:(h + 1) * vd] = o.astype(BF16)


def _attn_kernel(bounded_ref, pt_ref, qT_ref, kT_ref, v_ref, lam_ref, subln_ref, *rest,
                 n_seq, n_pages, tq, tk, rc, lam_init):
    del pt_ref
    qs_ref, kn_ref, vn_ref = rest[:3]
    page_refs = rest[3:3 + 2 * n_seq * n_pages]
    o_ref, os_ref, qs_scr, m_scr, acc_scr, s_scr, p_scr = rest[3 + 2 * n_seq * n_pages:]
    qi = pl.program_id(2)
    vd = v_ref.shape[2]
    lam = _lambda_full(lam_ref, lam_init)
    subln = subln_ref[...]

    q = qT_ref[0].T
    lane = lax.broadcasted_iota(jnp.int32, q.shape, 1)
    qs_scr[0:tq, :] = jnp.where(lane < HEAD_DIM, q, 0.0).astype(BF16)
    qs_scr[tq:2 * tq, :] = jnp.where(lane >= HEAD_DIM, q, 0.0).astype(BF16)
    acc_scr[...] = jnp.zeros(acc_scr.shape, F32)
    ones = jnp.ones((tk, vd), BF16)
    lower_tri = (lax.broadcasted_iota(jnp.int32, (rc, rc), 1)
                 <= lax.broadcasted_iota(jnp.int32, (rc, rc), 0))

    def keys(k0, n):
        kT = kT_ref[0, :, pl.ds(k0, n)]
        vv = jnp.concatenate([v_ref[0, pl.ds(k0, n), :], ones[:n]], axis=1)
        return kT, vv

    def update(r, kT, vv, bounded, triangle=False):
        s = jnp.dot(qs_scr[r:r + rc, :], kT, preferred_element_type=F32)
        if triangle:
            s = jnp.where(lower_tri, s, -jnp.inf)
        if bounded:
            pv = jnp.dot(jnp.exp2(s).astype(BF16), vv, preferred_element_type=F32)
            acc_scr[r:r + rc, :] += pv
        else:
            m_old = m_scr[r:r + rc, :]
            m_new = jnp.maximum(m_old, jnp.max(s, axis=1, keepdims=True))
            alpha = jnp.exp2(m_old - m_new)
            p = jnp.exp2(s - m_new)
            pv = jnp.dot(p.astype(BF16), vv, preferred_element_type=F32)
            acc_scr[r:r + rc, :] = acc_scr[r:r + rc, :] * alpha + pv
            m_scr[r:r + rc, :] = m_new

    def run(bounded):
        def body(jb, carry):
            kT, vv = keys(pl.multiple_of(jb * tk, tk), tk)
            for r in range(0, 2 * tq, rc):
                update(r, kT, vv, bounded)
            return carry

        lax.fori_loop(0, qi * (tq // tk), body, 0)
        base = pl.multiple_of(qi * tq, tq)
        for r in range(0, 2 * tq, rc):
            off = r % tq
            for c0 in range(0, off, tk):
                kT, vv = keys(pl.multiple_of(base + c0, rc), min(tk, off - c0))
                update(r, kT, vv, bounded)
            kT, vv = keys(pl.multiple_of(base + off, rc), rc)
            update(r, kT, vv, bounded, triangle=True)
        for i in range(n_seq):
            pages = page_refs[2 * i * n_pages:2 * (i + 1) * n_pages]
            _sample_attention(i, qs_ref, kn_ref, vn_ref, pages[:n_pages], pages[n_pages:], os_ref,
                              s_scr, p_scr, lam, subln, lam_init)
        acc = acc_scr[...]
        o = _diff_combine(acc[0:tq, 0:vd], acc[0:tq, vd:vd + 1], acc[tq:, 0:vd],
                          acc[tq:, vd:vd + 1], lam, subln, lam_init)
        o_ref[0] = o.astype(BF16)

    is_bounded = bounded_ref[0] == 1

    @pl.when(is_bounded)
    def _():
        run(True)

    @pl.when(jnp.logical_not(is_bounded))
    def _():
        m_scr[...] = jnp.full(m_scr.shape, -jnp.inf, F32)
        run(False)


def _attention(bounded, page_table_flat, qT, kT16, v16, q_s, kn_s, vn_s, lam_params, subln_all,
               cache_kT, cache_v, j, lam_init, *, tq, tk, rc):
    b, d, t = qT.shape
    bs, ts, _ = q_s.shape
    vd = 2 * HEAD_DIM
    n_heads = d // vd
    nq = t // tq
    n_steps = b * n_heads * nq
    n_seq = bs // n_steps
    assert n_seq * n_steps == bs, "sample sequences must split evenly over the prompt grid steps"
    n_pages = page_table_flat.shape[0] // bs
    n_hc = cache_kT.shape[2]

    def step(bi, hi, qi):
        return (bi * n_heads + hi) * nq + qi

    def k_map(i, p):
        return lambda bi, hi, qi, fl, pt: (j, pt[(step(bi, hi, qi) * n_seq + i) * n_pages + p],
                                           0, 0, 0)

    def v_map(i, p):
        return lambda bi, hi, qi, fl, pt: (j, pt[(step(bi, hi, qi) * n_seq + i) * n_pages + p],
                                           0, 0)

    seq_rows = pl.BlockSpec((n_seq, ts, d), lambda bi, hi, qi, fl, pt: (step(bi, hi, qi), 0, 0))
    sample_specs, sample_args = [seq_rows] * 3, [q_s, kn_s, vn_s]
    for i in range(n_seq):
        sample_specs += [pl.BlockSpec((1, 1, n_hc, HEAD_DIM, PAGE_SIZE), k_map(i, p))
                         for p in range(n_pages)]
        sample_specs += [pl.BlockSpec((None, None, cache_v.shape[2], vd), v_map(i, p))
                         for p in range(n_pages)]
        sample_args += [cache_kT] * n_pages + [cache_v] * n_pages
    grid_spec = pltpu.PrefetchScalarGridSpec(
        num_scalar_prefetch=2,
        grid=(b, n_heads, nq),
        in_specs=[
            pl.BlockSpec((1, vd, tq), lambda bi, hi, qi, fl, pt: (bi, hi, qi)),
            pl.BlockSpec((1, vd, t), lambda bi, hi, qi, fl, pt: (bi, hi, 0)),
            pl.BlockSpec((1, t, vd), lambda bi, hi, qi, fl, pt: (bi, 0, hi)),
            pl.BlockSpec((None, 4, HEAD_DIM), lambda bi, hi, qi, fl, pt: (j, 0, 0)),
            pl.BlockSpec((None, 1, vd), lambda bi, hi, qi, fl, pt: (j, 0, 0)),
        ] + sample_specs,
        out_specs=[pl.BlockSpec((1, tq, vd), lambda bi, hi, qi, fl, pt: (bi, qi, hi)), seq_rows],
        scratch_shapes=[
            pltpu.VMEM((2 * tq, vd), BF16),
            pltpu.VMEM((2 * tq, 1), F32),
            pltpu.VMEM((2 * tq, 2 * vd), F32),
            pltpu.VMEM((n_hc, ts, n_pages * PAGE_SIZE), F32),
            pltpu.VMEM((n_hc // 2, 2 * ts, n_pages * PAGE_SIZE), F32),
        ],
    )
    return pl.pallas_call(
        functools.partial(_attn_kernel, n_seq=n_seq, n_pages=n_pages, tq=tq, tk=tk, rc=rc,
                          lam_init=lam_init),
        grid_spec=grid_spec,
        out_shape=[jax.ShapeDtypeStruct((b, t, d), BF16), jax.ShapeDtypeStruct((bs, ts, d), BF16)],
        compiler_params=_cparams("arbitrary", "arbitrary", "arbitrary"),
        name="attention",
    )(bounded, page_table_flat, qT, kT16, v16, lam_params, subln_all, *sample_args)


def _scores_bounded(q_gain, k_gain):
    bound = (math.sqrt(HEAD_DIM) * math.log2(math.e)
             * jnp.max(jnp.abs(q_gain), axis=-1) * jnp.max(jnp.abs(k_gain), axis=-1))
    return (bound <= SCORE_BOUND_LOG2).astype(jnp.int32)


def _rope_tables(pos):
    inv = ROPE_THETA ** (-jnp.arange(HALF, dtype=F32) / HALF)
    ang = inv[:, None] * pos.astype(F32)[None, :]
    return jnp.cos(ang), jnp.sin(ang)


def kernel(x_prompt, x_sample, cache_k, cache_v, page_table, state_pool, norm_mix, norm_ffn, w_pool,
           pool_scale, w_qkv, q_norm, k_norm, lambda_q1, lambda_k1, lambda_q2, lambda_k2, subln, w_o,
           w_up, w_down):
    bp, sp, d = x_prompt.shape
    bs, ts, _ = x_sample.shape
    depth = norm_mix.shape[0]
    past_len = page_table.shape[1] * PAGE_SIZE
    n_hc = d // HEAD_DIM
    n_heads = n_hc // 2
    vd = 2 * HEAD_DIM

    w_up16 = w_up.astype(BF16)
    w_down16 = w_down.astype(BF16)
    w_pool16 = w_pool.astype(BF16)
    w_o16 = w_o.astype(BF16)
    wqkT16 = jnp.swapaxes(w_qkv[:, :, :2 * d], 1, 2).astype(BF16)
    wv16 = w_qkv[:, :, 2 * d:].astype(BF16)
    gq = q_norm[:, :, None]
    gk = k_norm[:, :, None]
    lam_params = jnp.stack([lambda_q1, lambda_k1, lambda_q2, lambda_k2], axis=1)
    norm_mix = norm_mix[:, None, :]
    norm_ffn = norm_ffn[:, None, :]
    pool_scale = pool_scale[:, None, :]
    subln = subln[:, None, :]

    cos_p, sin_p = _rope_tables(jnp.arange(sp))
    cos_s, sin_s = _rope_tables(past_len + jnp.tile(jnp.arange(ts), bs))

    cache_kT = jnp.transpose(cache_k, (0, 1, 3, 4, 2))
    state_tm = jnp.transpose(state_pool, (0, 2, 1, 3))
    cache_v2 = cache_v.reshape(cache_v.shape[0], cache_v.shape[1], PAGE_SIZE * n_heads, vd)
    pt_flat = page_table.reshape(-1)
    bounded = _scores_bounded(q_norm, k_norm)

    n_attn = depth // N_MIXERS
    mlp = functools.partial(_ffn, tm=FFN_ROWS, tf=FFN_COLS)
    xp = x_prompt.reshape(bp * sp, d)
    xs = x_sample.reshape(bs * ts, d)
    kv_p = None
    nk_s, nv_s, np_p, np_s = [], [], [], []
    for i in range(depth):
        j = i // N_MIXERS
        if i % N_MIXERS == 0:
            xp, pool_p = mlp(xp, norm_ffn, w_up16, w_down16, i,
                             pool=(norm_mix, w_pool16, pool_scale, j, sp))
            np_p.append(pool_p.reshape(bp, POOL_HALO, d)[:, POOL_HALO - POOL_BUF:])
            xs_tm = jnp.swapaxes(xs.reshape(bs, ts, d), 0, 1)
            xs_tm, pool_s = _pool_sample(xs_tm, state_tm, norm_mix, w_pool16, pool_scale, i, j,
                                         pos0=past_len, bb=32)
            xs = jnp.swapaxes(xs_tm, 0, 1).reshape(bs * ts, d)
            np_s.append(pool_s)
            xs = mlp(xs, norm_ffn, w_up16, w_down16, i)
        else:
            lam_init = 0.8 - 0.6 * math.exp(-0.3 * i)
            qT, kT32, kT16, v32, v16 = _qkv(xp.reshape(bp, sp, d), norm_mix, wqkT16, wv16, gq, gk,
                                            cos_p, sin_p, i, j, lambda bi, ti: (0, ti), tm=512,
                                            n_slots=n_attn, slot=j, prev=kv_p)
            kv_p = (kT32, v32)
            qT_s, kT32_s, _, v32_s, _ = _qkv(xs[None], norm_mix, wqkT16, wv16, gq, gk, cos_s, sin_s,
                                             i, j, lambda bi, ti: (0, ti), tm=512)
            q_s = qT_s[0].T.reshape(bs, ts, d)
            kn_s = kT32_s[0, 0].T.reshape(bs, ts, d)
            vn_s = v32_s.reshape(bs, ts, d)
            o_p, o_s = _attention(bounded[j:j + 1], pt_flat, qT, kT16, v16, q_s, kn_s, vn_s,
                                  lam_params, subln, cache_kT, cache_v2, j, lam_init,
                                  tq=1024, tk=1024, rc=256)
            xp = mlp(xp, norm_ffn, w_up16, w_down16, i, proj=(o_p.reshape(bp * sp, d), w_o16, j))
            xs = mlp(xs, norm_ffn, w_up16, w_down16, i, proj=(o_s.reshape(bs * ts, d), w_o16, j))
            nk_s.append(kn_s.reshape(bs, ts, n_hc, HEAD_DIM))
            nv_s.append(vn_s.reshape(bs, ts, n_heads, vd))

    kT32, v32 = kv_p
    new_k_prompt = jnp.transpose(kT32.reshape(n_attn, bp, n_hc, HEAD_DIM, sp), (0, 1, 4, 2, 3))
    new_v_prompt = v32.reshape(n_attn, bp, sp, n_heads, vd)
    new_k_sample = jnp.stack(nk_s)
    new_v_sample = jnp.stack(nv_s)
    new_pool_prompt = jnp.stack(np_p)
    new_pool_sample = jnp.transpose(jnp.stack(np_s), (0, 2, 1, 3))
    return (xp.reshape(bp, sp, d), xs.reshape(bs, ts, d), new_k_prompt, new_v_prompt, new_k_sample,
            new_v_sample, new_pool_prompt, new_pool_sample)
```

```python
import functools
import math

import jax
import jax.numpy as jnp
from jax import lax
from jax.experimental import pallas as pl
from jax.experimental.pallas import tpu as pltpu

F32 = jnp.float32
BF16 = jnp.bfloat16

N_MIXERS = 2
POOL_WINDOWS = (2, 4, 8, 16)
POOL_BUF = max(POOL_WINDOWS) - 1
POOL_HALO = 16
HEAD_DIM = 64
HALF = HEAD_DIM // 2
ROPE_THETA = 10000.0
EPS = 1e-6
SUBLN_EPS = 1e-5
PAGE_SIZE = 128
SCORE_BOUND_LOG2 = 48.0
Q_SCALE = math.log2(math.e) * HEAD_DIM ** -0.5
FFN_ROWS = 1024
FFN_COLS = 1024

V7X_VMEM_LIMIT_BYTES = 56 * 1024 * 1024


def _cparams(*sem):
    return pltpu.CompilerParams(dimension_semantics=sem, vmem_limit_bytes=V7X_VMEM_LIMIT_BYTES)


def _rms(x, g, eps=EPS):
    ms = jnp.mean(x * x, axis=-1, keepdims=True)
    return x * lax.rsqrt(ms + eps) * g


def _mlp_chunk(h_scr, wu_ref, wd_ref, o_ref):
    u = jnp.dot(h_scr[...], wu_ref[...], preferred_element_type=F32)
    a = jnp.maximum(u, 0.0)
    a = (a * a).astype(BF16)
    o_ref[...] += jnp.dot(a, wd_ref[...], preferred_element_type=F32)


def _ffn_kernel(x_ref, g_ref, wu_ref, wd_ref, o_ref, h_scr):
    @pl.when(pl.program_id(1) == 0)
    def _():
        x = x_ref[...]
        h_scr[...] = _rms(x, g_ref[...]).astype(BF16)
        o_ref[...] = x

    _mlp_chunk(h_scr, wu_ref, wd_ref, o_ref)


def _proj_ffn_kernel(x_ref, a_ref, wo_ref, g_ref, wu_ref, wd_ref, o_ref, h_scr):
    @pl.when(pl.program_id(1) == 0)
    def _():
        x = x_ref[...] + jnp.dot(a_ref[...], wo_ref[...], preferred_element_type=F32)
        h_scr[...] = _rms(x, g_ref[...]).astype(BF16)
        o_ref[...] = x

    _mlp_chunk(h_scr, wu_ref, wd_ref, o_ref)


def _pool_ffn_kernel(x_ref, prev_ref, gm_ref, wp_ref, sc_ref, g_ref, wu_ref, wd_ref, o_ref, np_ref,
                     h_scr, ext_scr, *, tm, tiles_per_seq):
    @pl.when(pl.program_id(1) == 0)
    def _():
        ti = pl.program_id(0) % tiles_per_seq
        gm = gm_ref[...]
        x = x_ref[...]
        h = _rms(x, gm)
        hp = _rms(prev_ref[...], gm)
        hp = jnp.where(ti == 0, 0.0, hp)
        ext_scr[0:POOL_HALO, :] = hp
        ext_scr[POOL_HALO:POOL_HALO + tm, :] = h
        pos1 = ti * tm + lax.broadcasted_iota(jnp.int32, (tm, 1), 0) + 1
        gw = x.shape[1] // len(POOL_WINDOWS)
        for gi, w in enumerate(POOL_WINDOWS):
            lo = gi * gw
            hg = h[:, lo:lo + gw]
            s = hg
            for j in range(1, w):
                s = s + ext_scr[POOL_HALO - j:POOL_HALO - j + tm, lo:lo + gw]
            cnt = jnp.minimum(pos1, w).astype(F32)
            dlt = (s / cnt - hg).astype(BF16)
            y = jnp.dot(dlt, wp_ref[gi], preferred_element_type=F32)
            o_ref[:, lo:lo + gw] = x[:, lo:lo + gw] + y * sc_ref[:, lo:lo + gw]
        h_scr[...] = _rms(o_ref[...], g_ref[...]).astype(BF16)

        @pl.when(ti == tiles_per_seq - 1)
        def _():
            np_ref[...] = ext_scr[tm:tm + POOL_HALO, :]

    _mlp_chunk(h_scr, wu_ref, wd_ref, o_ref)


def _ffn(x2d, g, w_up, w_down, layer, *, tm, tf, proj=None, pool=None):
    n, d = x2d.shape
    dff = w_up.shape[-1]
    row = pl.BlockSpec((tm, d), lambda i, c: (i, 0))
    mlp_specs = [
        pl.BlockSpec((None, 1, d), lambda i, c: (layer, 0, 0)),
        pl.BlockSpec((None, d, tf), lambda i, c: (layer, 0, c)),
        pl.BlockSpec((None, tf, d), lambda i, c: (layer, c, 0)),
    ]
    mlp_args = [g, w_up, w_down]
    out_specs, out_shape = row, jax.ShapeDtypeStruct((n, d), F32)
    scratch = [pltpu.VMEM((tm, d), BF16)]
    if proj is not None:
        a2d, w_o, j = proj
        body, name = _proj_ffn_kernel, "proj_ffn"
        in_specs = [row, row, pl.BlockSpec((None, d, d), lambda i, c: (j, 0, 0))] + mlp_specs
        args = [x2d, a2d, w_o] + mlp_args
    elif pool is not None:
        gm, w_pool, scale, j, seq_len = pool
        tiles_per_seq = seq_len // tm
        halo_blocks = tm // POOL_HALO
        n_groups = len(POOL_WINDOWS)
        gw = d // n_groups
        body = functools.partial(_pool_ffn_kernel, tm=tm, tiles_per_seq=tiles_per_seq)
        name = "pool_ffn"
        in_specs = [
            row,
            pl.BlockSpec((POOL_HALO, d), lambda i, c: (jnp.maximum(i * halo_blocks - 1, 0), 0)),
            pl.BlockSpec((None, 1, d), lambda i, c: (layer, 0, 0)),
            pl.BlockSpec((None, n_groups, gw, gw), lambda i, c: (j, 0, 0, 0)),
            pl.BlockSpec((None, 1, d), lambda i, c: (j, 0, 0)),
        ] + mlp_specs
        args = [x2d, x2d, gm, w_pool, scale] + mlp_args
        out_specs = [row, pl.BlockSpec((POOL_HALO, d), lambda i, c: (i // tiles_per_seq, 0))]
        out_shape = [out_shape,
                     jax.ShapeDtypeStruct((n // seq_len * POOL_HALO, d), F32)]
        scratch = scratch + [pltpu.VMEM((POOL_HALO + tm, d), F32)]
    else:
        body, name = _ffn_kernel, "ffn"
        in_specs = [row] + mlp_specs
        args = [x2d] + mlp_args
    return pl.pallas_call(
        body,
        grid=(n // tm, dff // tf),
        in_specs=in_specs,
        out_specs=out_specs,
        out_shape=out_shape,
        scratch_shapes=scratch,
        compiler_params=_cparams("parallel", "arbitrary"),
        name=name,
    )(*args)


def _pool_sample_kernel(x_ref, st_ref, g_ref, wp_ref, sc_ref, o_ref, np_ref, *, pos0):
    ts = x_ref.shape[0]
    g = g_ref[...]
    xs = [x_ref[t] for t in range(ts)]
    ext = [st_ref[r] for r in range(POOL_BUF)] + [_rms(xt, g) for xt in xs]
    gw = xs[0].shape[1] // len(POOL_WINDOWS)
    for t in range(ts):
        e = POOL_BUF + t
        for gi, w in enumerate(POOL_WINDOWS):
            lo = gi * gw
            s = ext[e][:, lo:lo + gw]
            for jj in range(1, w):
                s = s + ext[e - jj][:, lo:lo + gw]
            cnt = float(min(pos0 + t + 1, w))
            dlt = (s / cnt - ext[e][:, lo:lo + gw]).astype(BF16)
            y = jnp.dot(dlt, wp_ref[gi], preferred_element_type=F32)
            o_ref[t, :, lo:lo + gw] = xs[t][:, lo:lo + gw] + y * sc_ref[:, lo:lo + gw]
    for r in range(POOL_BUF):
        np_ref[r] = ext[ts + r]


def _pool_sample(x_tm, state_tm, g_all, w_pool, scale_all, layer, j, *, pos0, bb):
    ts, b, d = x_tm.shape
    n_groups = len(POOL_WINDOWS)
    gw = d // n_groups
    return pl.pallas_call(
        functools.partial(_pool_sample_kernel, pos0=pos0),
        grid=(b // bb,),
        in_specs=[
            pl.BlockSpec((ts, bb, d), lambda i: (0, i, 0)),
            pl.BlockSpec((None, POOL_BUF, bb, d), lambda i: (j, 0, i, 0)),
            pl.BlockSpec((None, 1, d), lambda i: (layer, 0, 0)),
            pl.BlockSpec((None, n_groups, gw, gw), lambda i: (j, 0, 0, 0)),
            pl.BlockSpec((None, 1, d), lambda i: (j, 0, 0)),
        ],
        out_specs=[
            pl.BlockSpec((ts, bb, d), lambda i: (0, i, 0)),
            pl.BlockSpec((POOL_BUF, bb, d), lambda i: (0, i, 0)),
        ],
        out_shape=[
            jax.ShapeDtypeStruct((ts, b, d), F32),
            jax.ShapeDtypeStruct((POOL_BUF, b, d), F32),
        ],
        compiler_params=_cparams("parallel"),
        name="pool_sample",
    )(x_tm, state_tm, g_all, w_pool, scale_all)


def _qkv_kernel(x_ref, g_ref, wqk_ref, wv_ref, gq_ref, gk_ref, cos_ref, sin_ref, *rest, q_scale):
    qT_ref, kT32_ref, kT16_ref, v32_ref, v16_ref = rest[-5:]
    x = x_ref[0]
    tm, d = x.shape
    h = _rms(x, g_ref[...]).astype(BF16)
    v = jnp.dot(h, wv_ref[...], preferred_element_type=F32)
    v32_ref[0] = v
    v16_ref[0] = v.astype(BF16)
    cos = cos_ref[...][None]
    sin = sin_ref[...][None]
    rows = 4 * HEAD_DIM
    for part in range(2):
        gn = (gq_ref if part == 0 else gk_ref)[...][None]
        for c in range(d // rows):
            r0 = part * d + c * rows
            yT = lax.dot_general(wqk_ref[r0:r0 + rows, :], h, (((1,), (1,)), ((), ())),
                                 preferred_element_type=F32)
            y3 = yT.reshape(rows // HEAD_DIM, HEAD_DIM, tm)
            ms = jnp.mean(y3 * y3, axis=1, keepdims=True)
            yn = y3 * lax.rsqrt(ms + EPS) * gn
            x1 = yn[:, :HALF, :]
            x2 = yn[:, HALF:, :]
            out = jnp.concatenate([x1 * cos - x2 * sin, x2 * cos + x1 * sin], axis=1)
            out = out.reshape(rows, tm)
            if part == 0:
                qT_ref[0, c * rows:(c + 1) * rows, :] = out * q_scale
            else:
                kT32_ref[0, c * rows:(c + 1) * rows, :] = out
                kT16_ref[0, c * rows:(c + 1) * rows, :] = out.astype(BF16)


def _qkv(x, g_all, wqkT, wv, gq, gk, cosT, sinT, layer, j, tab_map, *, tm, n_slots=1, slot=0,
         prev=None):
    b, t, d = x.shape
    tok = lambda bi, i: (bi, i, 0)
    tr = lambda bi, i: (bi, 0, i)
    in_specs = [
        pl.BlockSpec((1, tm, d), tok),
        pl.BlockSpec((None, 1, d), lambda bi, i: (layer, 0, 0)),
        pl.BlockSpec((None, 2 * d, d), lambda bi, i: (j, 0, 0)),
        pl.BlockSpec((None, d, d), lambda bi, i: (j, 0, 0)),
        pl.BlockSpec((None, HEAD_DIM, 1), lambda bi, i: (j, 0, 0)),
        pl.BlockSpec((None, HEAD_DIM, 1), lambda bi, i: (j, 0, 0)),
        pl.BlockSpec((HALF, tm), tab_map),
        pl.BlockSpec((HALF, tm), tab_map),
    ]
    args = [x, g_all, wqkT, wv, gq, gk, cosT, sinT]
    aliases = {}
    if prev is not None:
        aliases = {len(args): 1, len(args) + 1: 3}
        in_specs = in_specs + [pl.BlockSpec(memory_space=pl.ANY)] * 2
        args = args + list(prev)
    return pl.pallas_call(
        functools.partial(_qkv_kernel, q_scale=Q_SCALE),
        grid=(b, t // tm),
        in_specs=in_specs,
        out_specs=[
            pl.BlockSpec((1, d, tm), tr),
            pl.BlockSpec((None, 1, d, tm), lambda bi, i: (slot, bi, 0, i)),
            pl.BlockSpec((1, d, tm), tr),
            pl.BlockSpec((None, 1, tm, d), lambda bi, i: (slot, bi, i, 0)),
            pl.BlockSpec((1, tm, d), tok),
        ],
        out_shape=[
            jax.ShapeDtypeStruct((b, d, t), F32),
            jax.ShapeDtypeStruct((n_slots, b, d, t), F32),
            jax.ShapeDtypeStruct((b, d, t), BF16),
            jax.ShapeDtypeStruct((n_slots, b, t, d), F32),
            jax.ShapeDtypeStruct((b, t, d), BF16),
        ],
        input_output_aliases=aliases,
        compiler_params=_cparams("parallel", "parallel"),
        name="qkv",
    )(*args)


def _lambda_full(lam_ref, lam_init):
    lp = lam_ref[...]
    a = jnp.sum(lp[0:1] * lp[1:2], axis=-1, keepdims=True)
    b = jnp.sum(lp[2:3] * lp[3:4], axis=-1, keepdims=True)
    return jnp.exp(a) - jnp.exp(b) + lam_init


def _diff_combine(o1, l1, o2, l2, lam, subln, lam_init):
    o = o1 / l1 - lam * (o2 / l2)
    return _rms(o, subln, SUBLN_EPS) * (1.0 - lam_init)


def _sample_attention(i, q_ref, kn_ref, vn_ref, k_refs, v_refs, o_ref, s_scr, p_scr, lam, subln,
                      lam_init):
    n_pages = len(k_refs)
    q = q_ref[i]
    kn = kn_ref[i]
    vn = vn_ref[i]
    ts = q.shape[0]
    vd = 2 * HEAD_DIM
    n_heads = q.shape[1] // vd
    n_hc = 2 * n_heads
    causal = (lax.broadcasted_iota(jnp.int32, (ts, ts), 1)
              <= lax.broadcasted_iota(jnp.int32, (ts, ts), 0))
    nt = (((1,), (1,)), ((), ()))
    qh = [q[:, hc * HEAD_DIM:(hc + 1) * HEAD_DIM] for hc in range(n_hc)]
    for hc in range(n_hc):
        for p in range(0, n_pages, 2):
            k2 = jnp.concatenate([k_refs[p][0, 0, hc], k_refs[p + 1][0, 0, hc]], axis=1)
            s_scr[hc, :, p * PAGE_SIZE:(p + 2) * PAGE_SIZE] = jnp.dot(
                qh[hc], k2, preferred_element_type=F32)
    p_new, row_sum = [], []
    for hc in range(n_hc):
        s = s_scr[hc]
        s_new = lax.dot_general(qh[hc], kn[:, hc * HEAD_DIM:(hc + 1) * HEAD_DIM], nt,
                                preferred_element_type=F32)
        s_new = jnp.where(causal, s_new, -jnp.inf)
        m = jnp.maximum(jnp.max(s, axis=1, keepdims=True), jnp.max(s_new, axis=1, keepdims=True))
        pr = jnp.exp2(s - m)
        pn = jnp.exp2(s_new - m)
        p_scr[hc // 4, (hc % 4) * ts:(hc % 4 + 1) * ts, :] = pr
        p_new.append(pn)
        row_sum.append(jnp.sum(pr, axis=1, keepdims=True) + jnp.sum(pn, axis=1, keepdims=True))
    for h in range(0, n_heads, 2):
        pn4 = jnp.concatenate(p_new[2 * h:2 * h + 4], axis=0)
        acc = jnp.dot(pn4, vn[:, h * vd:(h + 2) * vd], preferred_element_type=F32)
        for p in range(n_pages):
            v2 = jnp.concatenate([v_refs[p][pl.ds(h, PAGE_SIZE, stride=n_heads), :],
                                  v_refs[p][pl.ds(h + 1, PAGE_SIZE, stride=n_heads), :]], axis=1)
            acc = acc + jnp.dot(p_scr[h // 2, :, p * PAGE_SIZE:(p + 1) * PAGE_SIZE], v2,
                                preferred_element_type=F32)
        for a in range(2):
            blk = acc[2 * a * ts:2 * (a + 1) * ts, a * vd:(a + 1) * vd]
            o = _diff_combine(blk[:ts], row_sum[2 * (h + a)], blk[ts:], row_sum[2 * (h + a) + 1],
                              lam, subln, lam_init)
            o_ref[i, :, (h + a) * vd:(h + a + 1) * vd] = o.astype(BF16)


def _attn_kernel(bounded_ref, pt_ref, qT_ref, kT_ref, v_ref, lam_ref, subln_ref, *rest,
                 n_seq, n_pages, tq, tk, rc, lam_init):
    del pt_ref
    qs_ref, kn_ref, vn_ref = rest[:3]
    page_refs = rest[3:3 + 2 * n_seq * n_pages]
    o_ref, os_ref, qs_scr, m_scr, acc_scr, s_scr, p_scr = rest[3 + 2 * n_seq * n_pages:]
    qi = pl.program_id(2)
    vd = v_ref.shape[2]
    lam = _lambda_full(lam_ref, lam_init)
    subln = subln_ref[...]

    q = qT_ref[0].T
    lane = lax.broadcasted_iota(jnp.int32, q.shape, 1)
    qs_scr[0:tq, :] = jnp.where(lane < HEAD_DIM, q, 0.0).astype(BF16)
    qs_scr[tq:2 * tq, :] = jnp.where(lane >= HEAD_DIM, q, 0.0).astype(BF16)
    acc_scr[...] = jnp.zeros(acc_scr.shape, F32)
    ones = jnp.ones((max(tk, tq), vd), BF16)
    lower_tri = (lax.broadcasted_iota(jnp.int32, (rc, rc), 1)
                 <= lax.broadcasted_iota(jnp.int32, (rc, rc), 0))

    def keys(k0, n):
        kT = kT_ref[0, :, pl.ds(k0, n)]
        vv = jnp.concatenate([v_ref[0, pl.ds(k0, n), :], ones[:n]], axis=1)
        return kT, vv

    def update(r, kT, vv, bounded, triangle=False):
        s = jnp.dot(qs_scr[r:r + rc, :], kT, preferred_element_type=F32)
        if triangle:
            n = s.shape[1]
            tri = jnp.where(lower_tri, s[:, n - rc:], -jnp.inf)
            s = tri if n == rc else jnp.concatenate([s[:, :n - rc], tri], axis=1)
        if bounded:
            pv = jnp.dot(jnp.exp2(s).astype(BF16), vv, preferred_element_type=F32)
            acc_scr[r:r + rc, :] += pv
        else:
            m_old = m_scr[r:r + rc, :]
            m_new = jnp.maximum(m_old, jnp.max(s, axis=1, keepdims=True))
            alpha = jnp.exp2(m_old - m_new)
            p = jnp.exp2(s - m_new)
            pv = jnp.dot(p.astype(BF16), vv, preferred_element_type=F32)
            acc_scr[r:r + rc, :] = acc_scr[r:r + rc, :] * alpha + pv
            m_scr[r:r + rc, :] = m_new

    def run(bounded):
        def body(jb, carry):
            kT, vv = keys(pl.multiple_of(jb * tk, tk), tk)
            for r in range(0, 2 * tq, rc):
                update(r, kT, vv, bounded)
            return carry

        lax.fori_loop(0, qi * (tq // tk), body, 0)
        for i in range(n_seq):
            pages = page_refs[2 * i * n_pages:2 * (i + 1) * n_pages]
            _sample_attention(i, qs_ref, kn_ref, vn_ref, pages[:n_pages], pages[n_pages:], os_ref,
                              s_scr, p_scr, lam, subln, lam_init)
        base = pl.multiple_of(qi * tq, tq)
        for r in range(0, 2 * tq, rc):
            kT, vv = keys(base, (r % tq) + rc)
            update(r, kT, vv, bounded, triangle=True)
        acc = acc_scr[...]
        o = _diff_combine(acc[0:tq, 0:vd], acc[0:tq, vd:vd + 1], acc[tq:, 0:vd],
                          acc[tq:, vd:vd + 1], lam, subln, lam_init)
        o_ref[0] = o.astype(BF16)

    is_bounded = bounded_ref[0] == 1

    @pl.when(is_bounded)
    def _():
        run(True)

    @pl.when(jnp.logical_not(is_bounded))
    def _():
        m_scr[...] = jnp.full(m_scr.shape, -jnp.inf, F32)
        run(False)


def _attention(bounded, page_table_flat, qT, kT16, v16, q_s, kn_s, vn_s, lam_params, subln_all,
               cache_kT, cache_v, j, lam_init, *, tq, tk, rc):
    b, d, t = qT.shape
    bs, ts, _ = q_s.shape
    vd = 2 * HEAD_DIM
    n_heads = d // vd
    nq = t // tq
    n_steps = b * n_heads * nq
    n_seq = bs // n_steps
    assert n_seq * n_steps == bs, "sample sequences must split evenly over the prompt grid steps"
    n_pages = page_table_flat.shape[0] // bs
    n_hc = cache_kT.shape[2]

    def step(bi, hi, qi):
        return (bi * n_heads + hi) * nq + qi

    def k_map(i, p):
        return lambda bi, hi, qi, fl, pt: (j, pt[(step(bi, hi, qi) * n_seq + i) * n_pages + p],
                                           0, 0, 0)

    def v_map(i, p):
        return lambda bi, hi, qi, fl, pt: (j, pt[(step(bi, hi, qi) * n_seq + i) * n_pages + p],
                                           0, 0)

    seq_rows = pl.BlockSpec((n_seq, ts, d), lambda bi, hi, qi, fl, pt: (step(bi, hi, qi), 0, 0))
    sample_specs, sample_args = [seq_rows] * 3, [q_s, kn_s, vn_s]
    for i in range(n_seq):
        sample_specs += [pl.BlockSpec((1, 1, n_hc, HEAD_DIM, PAGE_SIZE), k_map(i, p))
                         for p in range(n_pages)]
        sample_specs += [pl.BlockSpec((None, None, cache_v.shape[2], vd), v_map(i, p))
                         for p in range(n_pages)]
        sample_args += [cache_kT] * n_pages + [cache_v] * n_pages
    grid_spec = pltpu.PrefetchScalarGridSpec(
        num_scalar_prefetch=2,
        grid=(b, n_heads, nq),
        in_specs=[
            pl.BlockSpec((1, vd, tq), lambda bi, hi, qi, fl, pt: (bi, hi, qi)),
            pl.BlockSpec((1, vd, t), lambda bi, hi, qi, fl, pt: (bi, hi, 0)),
            pl.BlockSpec((1, t, vd), lambda bi, hi, qi, fl, pt: (bi, 0, hi)),
            pl.BlockSpec((None, 4, HEAD_DIM), lambda bi, hi, qi, fl, pt: (j, 0, 0)),
            pl.BlockSpec((None, 1, vd), lambda bi, hi, qi, fl, pt: (j, 0, 0)),
        ] + sample_specs,
        out_specs=[pl.BlockSpec((1, tq, vd), lambda bi, hi, qi, fl, pt: (bi, qi, hi)), seq_rows],
        scratch_shapes=[
            pltpu.VMEM((2 * tq, vd), BF16),
            pltpu.VMEM((2 * tq, 1), F32),
            pltpu.VMEM((2 * tq, 2 * vd), F32),
            pltpu.VMEM((n_hc, ts, n_pages * PAGE_SIZE), F32),
            pltpu.VMEM((n_hc // 4, 4 * ts, n_pages * PAGE_SIZE), F32),
        ],
    )
    return pl.pallas_call(
        functools.partial(_attn_kernel, n_seq=n_seq, n_pages=n_pages, tq=tq, tk=tk, rc=rc,
                          lam_init=lam_init),
        grid_spec=grid_spec,
        out_shape=[jax.ShapeDtypeStruct((b, t, d), BF16), jax.ShapeDtypeStruct((bs, ts, d), BF16)],
        compiler_params=_cparams("arbitrary", "arbitrary", "arbitrary"),
        name="attention",
    )(bounded, page_table_flat, qT, kT16, v16, lam_params, subln_all, *sample_args)


def _scores_bounded(q_gain, k_gain):
    bound = (math.sqrt(HEAD_DIM) * math.log2(math.e)
             * jnp.max(jnp.abs(q_gain), axis=-1) * jnp.max(jnp.abs(k_gain), axis=-1))
    return (bound <= SCORE_BOUND_LOG2).astype(jnp.int32)


def _rope_tables(pos):
    inv = ROPE_THETA ** (-jnp.arange(HALF, dtype=F32) / HALF)
    ang = inv[:, None] * pos.astype(F32)[None, :]
    return jnp.cos(ang), jnp.sin(ang)


def kernel(x_prompt, x_sample, cache_k, cache_v, page_table, state_pool, norm_mix, norm_ffn, w_pool,
           pool_scale, w_qkv, q_norm, k_norm, lambda_q1, lambda_k1, lambda_q2, lambda_k2, subln, w_o,
           w_up, w_down):
    bp, sp, d = x_prompt.shape
    bs, ts, _ = x_sample.shape
    depth = norm_mix.shape[0]
    past_len = page_table.shape[1] * PAGE_SIZE
    n_hc = d // HEAD_DIM
    n_heads = n_hc // 2
    vd = 2 * HEAD_DIM

    w_up16 = w_up.astype(BF16)
    w_down16 = w_down.astype(BF16)
    w_pool16 = w_pool.astype(BF16)
    w_o16 = w_o.astype(BF16)
    wqkT16 = jnp.swapaxes(w_qkv[:, :, :2 * d], 1, 2).astype(BF16)
    wv16 = w_qkv[:, :, 2 * d:].astype(BF16)
    gq = q_norm[:, :, None]
    gk = k_norm[:, :, None]
    lam_params = jnp.stack([lambda_q1, lambda_k1, lambda_q2, lambda_k2], axis=1)
    norm_mix = norm_mix[:, None, :]
    norm_ffn = norm_ffn[:, None, :]
    pool_scale = pool_scale[:, None, :]
    subln = subln[:, None, :]

    cos_p, sin_p = _rope_tables(jnp.arange(sp))
    cos_s, sin_s = _rope_tables(past_len + jnp.tile(jnp.arange(ts), bs))

    cache_kT = jnp.transpose(cache_k, (0, 1, 3, 4, 2))
    state_tm = jnp.transpose(state_pool, (0, 2, 1, 3))
    cache_v2 = cache_v.reshape(cache_v.shape[0], cache_v.shape[1], PAGE_SIZE * n_heads, vd)
    pt_flat = page_table.reshape(-1)
    bounded = _scores_bounded(q_norm, k_norm)

    n_attn = depth // N_MIXERS
    mlp = functools.partial(_ffn, tm=FFN_ROWS, tf=FFN_COLS)
    xp = x_prompt.reshape(bp * sp, d)
    xs = x_sample.reshape(bs * ts, d)
    kv_p = None
    nk_s, nv_s, np_p, np_s = [], [], [], []
    for i in range(depth):
        j = i // N_MIXERS
        if i % N_MIXERS == 0:
            xp, pool_p = mlp(xp, norm_ffn, w_up16, w_down16, i,
                             pool=(norm_mix, w_pool16, pool_scale, j, sp))
            np_p.append(pool_p.reshape(bp, POOL_HALO, d)[:, POOL_HALO - POOL_BUF:])
            xs_tm = jnp.swapaxes(xs.reshape(bs, ts, d), 0, 1)
            xs_tm, pool_s = _pool_sample(xs_tm, state_tm, norm_mix, w_pool16, pool_scale, i, j,
                                         pos0=past_len, bb=32)
            xs = jnp.swapaxes(xs_tm, 0, 1).reshape(bs * ts, d)
            np_s.append(pool_s)
            xs = mlp(xs, norm_ffn, w_up16, w_down16, i)
        else:
            lam_init = 0.8 - 0.6 * math.exp(-0.3 * i)
            qT, kT32, kT16, v32, v16 = _qkv(xp.reshape(bp, sp, d), norm_mix, wqkT16, wv16, gq, gk,
                                            cos_p, sin_p, i, j, lambda bi, ti: (0, ti), tm=512,
                                            n_slots=n_attn, slot=j, prev=kv_p)
            kv_p = (kT32, v32)
            qT_s, kT32_s, _, v32_s, _ = _qkv(xs[None], norm_mix, wqkT16, wv16, gq, gk, cos_s, sin_s,
                                             i, j, lambda bi, ti: (0, ti), tm=512)
            q_s = qT_s[0].T.reshape(bs, ts, d)
            kn_s = kT32_s[0, 0].T.reshape(bs, ts, d)
            vn_s = v32_s.reshape(bs, ts, d)
            o_p, o_s = _attention(bounded[j:j + 1], pt_flat, qT, kT16, v16, q_s, kn_s, vn_s,
                                  lam_params, subln, cache_kT, cache_v2, j, lam_init,
                                  tq=1024, tk=1024, rc=256)
            xp = mlp(xp, norm_ffn, w_up16, w_down16, i, proj=(o_p.reshape(bp * sp, d), w_o16, j))
            xs = mlp(xs, norm_ffn, w_up16, w_down16, i, proj=(o_s.reshape(bs * ts, d), w_o16, j))
            nk_s.append(kn_s.reshape(bs, ts, n_hc, HEAD_DIM))
            nv_s.append(vn_s.reshape(bs, ts, n_heads, vd))

    kT32, v32 = kv_p
    new_k_prompt = jnp.transpose(kT32.reshape(n_attn, bp, n_hc, HEAD_DIM, sp), (0, 1, 4, 2, 3))
    new_v_prompt = v32.reshape(n_attn, bp, sp, n_heads, vd)
    new_k_sample = jnp.stack(nk_s)
    new_v_sample = jnp.stack(nv_s)
    new_pool_prompt = jnp.stack(np_p)
    new_pool_sample = jnp.transpose(jnp.stack(np_s), (0, 2, 1, 3))
    return (xp.reshape(bp, sp, d), xs.reshape(bs, ts, d), new_k_prompt, new_v_prompt, new_k_sample,
            new_v_sample, new_pool_prompt, new_pool_sample)
```

```python
import functools
import math

import jax
import jax.numpy as jnp
from jax import lax
from jax.experimental import pallas as pl
from jax.experimental.pallas import tpu as pltpu

F32 = jnp.float32
BF16 = jnp.bfloat16

N_MIXERS = 2
POOL_WINDOWS = (2, 4, 8, 16)
POOL_BUF = max(POOL_WINDOWS) - 1
POOL_HALO = 16
HEAD_DIM = 64
HALF = HEAD_DIM // 2
ROPE_THETA = 10000.0
EPS = 1e-6
SUBLN_EPS = 1e-5
PAGE_SIZE = 128
SCORE_BOUND_LOG2 = 48.0
Q_SCALE = math.log2(math.e) * HEAD_DIM ** -0.5
FFN_ROWS = 1024
FFN_COLS = 1024

V7X_VMEM_LIMIT_BYTES = 56 * 1024 * 1024


def _cparams(*sem):
    return pltpu.CompilerParams(dimension_semantics=sem, vmem_limit_bytes=V7X_VMEM_LIMIT_BYTES)


def _rms(x, g, eps=EPS):
    ms = jnp.mean(x * x, axis=-1, keepdims=True)
    return x * lax.rsqrt(ms + eps) * g


def _mlp_chunk(h_scr, wu_ref, wd_ref, o_ref):
    u = jnp.dot(h_scr[...], wu_ref[...], preferred_element_type=F32)
    a = jnp.maximum(u, 0.0)
    a = (a * a).astype(BF16)
    o_ref[...] += jnp.dot(a, wd_ref[...], preferred_element_type=F32)


def _ffn_kernel(x_ref, g_ref, wu_ref, wd_ref, o_ref, h_scr):
    @pl.when(pl.program_id(1) == 0)
    def _():
        x = x_ref[...]
        h_scr[...] = _rms(x, g_ref[...]).astype(BF16)
        o_ref[...] = x

    _mlp_chunk(h_scr, wu_ref, wd_ref, o_ref)


def _proj_ffn_kernel(x_ref, a_ref, wo_ref, g_ref, wu_ref, wd_ref, o_ref, h_scr):
    @pl.when(pl.program_id(1) == 0)
    def _():
        x = x_ref[...] + jnp.dot(a_ref[...], wo_ref[...], preferred_element_type=F32)
        h_scr[...] = _rms(x, g_ref[...]).astype(BF16)
        o_ref[...] = x

    _mlp_chunk(h_scr, wu_ref, wd_ref, o_ref)


def _pool_ffn_kernel(x_ref, prev_ref, gm_ref, wp_ref, sc_ref, g_ref, wu_ref, wd_ref, o_ref, np_ref,
                     h_scr, ext_scr, *, tm, tiles_per_seq):
    @pl.when(pl.program_id(1) == 0)
    def _():
        ti = pl.program_id(0) % tiles_per_seq
        gm = gm_ref[...]
        x = x_ref[...]
        h = _rms(x, gm)
        hp = _rms(prev_ref[...], gm)
        hp = jnp.where(ti == 0, 0.0, hp)
        ext_scr[0:POOL_HALO, :] = hp
        ext_scr[POOL_HALO:POOL_HALO + tm, :] = h
        pos1 = ti * tm + lax.broadcasted_iota(jnp.int32, (tm, 1), 0) + 1
        gw = x.shape[1] // len(POOL_WINDOWS)
        for gi, w in enumerate(POOL_WINDOWS):
            lo = gi * gw
            hg = h[:, lo:lo + gw]
            s = hg
            for j in range(1, w):
                s = s + ext_scr[POOL_HALO - j:POOL_HALO - j + tm, lo:lo + gw]
            cnt = jnp.minimum(pos1, w).astype(F32)
            dlt = (s / cnt - hg).astype(BF16)
            y = jnp.dot(dlt, wp_ref[gi], preferred_element_type=F32)
            o_ref[:, lo:lo + gw] = x[:, lo:lo + gw] + y * sc_ref[:, lo:lo + gw]
        h_scr[...] = _rms(o_ref[...], g_ref[...]).astype(BF16)

        @pl.when(ti == tiles_per_seq - 1)
        def _():
            np_ref[...] = ext_scr[tm:tm + POOL_HALO, :]

    _mlp_chunk(h_scr, wu_ref, wd_ref, o_ref)


def _ffn(x2d, g, w_up, w_down, layer, *, tm, tf, proj=None, pool=None):
    n, d = x2d.shape
    dff = w_up.shape[-1]
    row = pl.BlockSpec((tm, d), lambda i, c: (i, 0))
    mlp_specs = [
        pl.BlockSpec((None, 1, d), lambda i, c: (layer, 0, 0)),
        pl.BlockSpec((None, d, tf), lambda i, c: (layer, 0, c)),
        pl.BlockSpec((None, tf, d), lambda i, c: (layer, c, 0)),
    ]
    mlp_args = [g, w_up, w_down]
    out_specs, out_shape = row, jax.ShapeDtypeStruct((n, d), F32)
    scratch = [pltpu.VMEM((tm, d), BF16)]
    if proj is not None:
        a2d, w_o, j = proj
        body, name = _proj_ffn_kernel, "proj_ffn"
        in_specs = [row, row, pl.BlockSpec((None, d, d), lambda i, c: (j, 0, 0))] + mlp_specs
        args = [x2d, a2d, w_o] + mlp_args
    elif pool is not None:
        gm, w_pool, scale, j, seq_len = pool
        tiles_per_seq = seq_len // tm
        halo_blocks = tm // POOL_HALO
        n_groups = len(POOL_WINDOWS)
        gw = d // n_groups
        body = functools.partial(_pool_ffn_kernel, tm=tm, tiles_per_seq=tiles_per_seq)
        name = "pool_ffn"
        in_specs = [
            row,
            pl.BlockSpec((POOL_HALO, d), lambda i, c: (jnp.maximum(i * halo_blocks - 1, 0), 0)),
            pl.BlockSpec((None, 1, d), lambda i, c: (layer, 0, 0)),
            pl.BlockSpec((None, n_groups, gw, gw), lambda i, c: (j, 0, 0, 0)),
            pl.BlockSpec((None, 1, d), lambda i, c: (j, 0, 0)),
        ] + mlp_specs
        args = [x2d, x2d, gm, w_pool, scale] + mlp_args
        out_specs = [row, pl.BlockSpec((POOL_HALO, d), lambda i, c: (i // tiles_per_seq, 0))]
        out_shape = [out_shape,
                     jax.ShapeDtypeStruct((n // seq_len * POOL_HALO, d), F32)]
        scratch = scratch + [pltpu.VMEM((POOL_HALO + tm, d), F32)]
    else:
        body, name = _ffn_kernel, "ffn"
        in_specs = [row] + mlp_specs
        args = [x2d] + mlp_args
    return pl.pallas_call(
        body,
        grid=(n // tm, dff // tf),
        in_specs=in_specs,
        out_specs=out_specs,
        out_shape=out_shape,
        scratch_shapes=scratch,
        compiler_params=_cparams("parallel", "arbitrary"),
        name=name,
    )(*args)


def _pool_sample_kernel(x_ref, st_ref, g_ref, wp_ref, sc_ref, o_ref, np_ref, *, pos0):
    ts = x_ref.shape[0]
    g = g_ref[...]
    xs = [x_ref[t] for t in range(ts)]
    ext = [st_ref[r] for r in range(POOL_BUF)] + [_rms(xt, g) for xt in xs]
    gw = xs[0].shape[1] // len(POOL_WINDOWS)
    for t in range(ts):
        e = POOL_BUF + t
        for gi, w in enumerate(POOL_WINDOWS):
            lo = gi * gw
            s = ext[e][:, lo:lo + gw]
            for jj in range(1, w):
                s = s + ext[e - jj][:, lo:lo + gw]
            cnt = float(min(pos0 + t + 1, w))
            dlt = (s / cnt - ext[e][:, lo:lo + gw]).astype(BF16)
            y = jnp.dot(dlt, wp_ref[gi], preferred_element_type=F32)
            o_ref[t, :, lo:lo + gw] = xs[t][:, lo:lo + gw] + y * sc_ref[:, lo:lo + gw]
    for r in range(POOL_BUF):
        np_ref[r] = ext[ts + r]


def _pool_sample(x_tm, state_tm, g_all, w_pool, scale_all, layer, j, *, pos0, bb):
    ts, b, d = x_tm.shape
    n_groups = len(POOL_WINDOWS)
    gw = d // n_groups
    return pl.pallas_call(
        functools.partial(_pool_sample_kernel, pos0=pos0),
        grid=(b // bb,),
        in_specs=[
            pl.BlockSpec((ts, bb, d), lambda i: (0, i, 0)),
            pl.BlockSpec((None, POOL_BUF, bb, d), lambda i: (j, 0, i, 0)),
            pl.BlockSpec((None, 1, d), lambda i: (layer, 0, 0)),
            pl.BlockSpec((None, n_groups, gw, gw), lambda i: (j, 0, 0, 0)),
            pl.BlockSpec((None, 1, d), lambda i: (j, 0, 0)),
        ],
        out_specs=[
            pl.BlockSpec((ts, bb, d), lambda i: (0, i, 0)),
            pl.BlockSpec((POOL_BUF, bb, d), lambda i: (0, i, 0)),
        ],
        out_shape=[
            jax.ShapeDtypeStruct((ts, b, d), F32),
            jax.ShapeDtypeStruct((POOL_BUF, b, d), F32),
        ],
        compiler_params=_cparams("parallel"),
        name="pool_sample",
    )(x_tm, state_tm, g_all, w_pool, scale_all)


def _qkv_kernel(x_ref, g_ref, wqk_ref, wv_ref, gq_ref, gk_ref, cos_ref, sin_ref, *rest, q_scale,
                write_slot):
    qT_ref, kT32_ref, kT16_ref, v32_ref, v16_ref = rest[-5:]
    for w in range(kT32_ref.shape[0]):
        if w != write_slot:
            kT32_ref[w] = jnp.zeros(kT32_ref.shape[1:], F32)
            v32_ref[w] = jnp.zeros(v32_ref.shape[1:], F32)
    x = x_ref[0]
    tm, d = x.shape
    h = _rms(x, g_ref[...]).astype(BF16)
    v = jnp.dot(h, wv_ref[...], preferred_element_type=F32)
    v32_ref[write_slot, 0] = v
    v16_ref[0] = v.astype(BF16)
    cos = cos_ref[...][None]
    sin = sin_ref[...][None]
    rows = 4 * HEAD_DIM
    for part in range(2):
        gn = (gq_ref if part == 0 else gk_ref)[...][None]
        for c in range(d // rows):
            r0 = part * d + c * rows
            yT = lax.dot_general(wqk_ref[r0:r0 + rows, :], h, (((1,), (1,)), ((), ())),
                                 preferred_element_type=F32)
            y3 = yT.reshape(rows // HEAD_DIM, HEAD_DIM, tm)
            ms = jnp.mean(y3 * y3, axis=1, keepdims=True)
            yn = y3 * lax.rsqrt(ms + EPS) * gn
            x1 = yn[:, :HALF, :]
            x2 = yn[:, HALF:, :]
            out = jnp.concatenate([x1 * cos - x2 * sin, x2 * cos + x1 * sin], axis=1)
            out = out.reshape(rows, tm)
            if part == 0:
                qT_ref[0, c * rows:(c + 1) * rows, :] = out * q_scale
            else:
                kT32_ref[write_slot, 0, c * rows:(c + 1) * rows, :] = out
                kT16_ref[0, c * rows:(c + 1) * rows, :] = out.astype(BF16)


def _qkv(x, g_all, wqkT, wv, gq, gk, cosT, sinT, layer, j, tab_map, *, tm, n_slots=1, slot=0,
         prev=None):
    b, t, d = x.shape
    tok = lambda bi, i: (bi, i, 0)
    tr = lambda bi, i: (bi, 0, i)
    in_specs = [
        pl.BlockSpec((1, tm, d), tok),
        pl.BlockSpec((None, 1, d), lambda bi, i: (layer, 0, 0)),
        pl.BlockSpec((None, 2 * d, d), lambda bi, i: (j, 0, 0)),
        pl.BlockSpec((None, d, d), lambda bi, i: (j, 0, 0)),
        pl.BlockSpec((None, HEAD_DIM, 1), lambda bi, i: (j, 0, 0)),
        pl.BlockSpec((None, HEAD_DIM, 1), lambda bi, i: (j, 0, 0)),
        pl.BlockSpec((HALF, tm), tab_map),
        pl.BlockSpec((HALF, tm), tab_map),
    ]
    args = [x, g_all, wqkT, wv, gq, gk, cosT, sinT]
    aliases = {}
    n_blk, blk, write_slot = n_slots, 0, slot
    if prev is not None:
        aliases = {len(args): 1, len(args) + 1: 3}
        in_specs = in_specs + [pl.BlockSpec(memory_space=pl.ANY)] * 2
        args = args + list(prev)
        n_blk, blk, write_slot = 1, slot, 0
    return pl.pallas_call(
        functools.partial(_qkv_kernel, q_scale=Q_SCALE, write_slot=write_slot),
        grid=(b, t // tm),
        in_specs=in_specs,
        out_specs=[
            pl.BlockSpec((1, d, tm), tr),
            pl.BlockSpec((n_blk, 1, d, tm), lambda bi, i: (blk, bi, 0, i)),
            pl.BlockSpec((1, d, tm), tr),
            pl.BlockSpec((n_blk, 1, tm, d), lambda bi, i: (blk, bi, i, 0)),
            pl.BlockSpec((1, tm, d), tok),
        ],
        out_shape=[
            jax.ShapeDtypeStruct((b, d, t), F32),
            jax.ShapeDtypeStruct((n_slots, b, d, t), F32),
            jax.ShapeDtypeStruct((b, d, t), BF16),
            jax.ShapeDtypeStruct((n_slots, b, t, d), F32),
            jax.ShapeDtypeStruct((b, t, d), BF16),
        ],
        input_output_aliases=aliases,
        compiler_params=_cparams("parallel", "parallel"),
        name="qkv",
    )(*args)


def _lambda_full(lam_ref, lam_init):
    lp = lam_ref[...]
    a = jnp.sum(lp[0:1] * lp[1:2], axis=-1, keepdims=True)
    b = jnp.sum(lp[2:3] * lp[3:4], axis=-1, keepdims=True)
    return jnp.exp(a) - jnp.exp(b) + lam_init


def _diff_combine(o1, l1, o2, l2, lam, subln, lam_init):
    o = o1 / l1 - lam * (o2 / l2)
    return _rms(o, subln, SUBLN_EPS) * (1.0 - lam_init)


def _sample_attention(i, q_ref, kn_ref, vn_ref, k_page, v_page, n_pages, o_ref, s_scr, p_scr, lam,
                      subln, lam_init):
    q = q_ref[i]
    kn = kn_ref[i]
    vn = vn_ref[i]
    ts = q.shape[0]
    vd = 2 * HEAD_DIM
    n_heads = q.shape[1] // vd
    n_hc = 2 * n_heads
    causal = (lax.broadcasted_iota(jnp.int32, (ts, ts), 1)
              <= lax.broadcasted_iota(jnp.int32, (ts, ts), 0))
    nt = (((1,), (1,)), ((), ()))
    qh = [q[:, hc * HEAD_DIM:(hc + 1) * HEAD_DIM] for hc in range(n_hc)]
    for hc in range(n_hc):
        for p in range(0, n_pages, 2):
            k2 = jnp.concatenate([k_page(p, hc), k_page(p + 1, hc)], axis=1)
            s_scr[hc, :, p * PAGE_SIZE:(p + 2) * PAGE_SIZE] = jnp.dot(
                qh[hc], k2, preferred_element_type=F32)
    p_new, row_sum = [], []
    for hc in range(n_hc):
        s = s_scr[hc]
        s_new = lax.dot_general(qh[hc], kn[:, hc * HEAD_DIM:(hc + 1) * HEAD_DIM], nt,
                                preferred_element_type=F32)
        s_new = jnp.where(causal, s_new, -jnp.inf)
        m = jnp.maximum(jnp.max(s, axis=1, keepdims=True), jnp.max(s_new, axis=1, keepdims=True))
        pr = jnp.exp2(s - m)
        pn = jnp.exp2(s_new - m)
        p_scr[hc // 4, (hc % 4) * ts:(hc % 4 + 1) * ts, :] = pr
        p_new.append(pn)
        row_sum.append(jnp.sum(pr, axis=1, keepdims=True) + jnp.sum(pn, axis=1, keepdims=True))
    for h in range(0, n_heads, 2):
        pn4 = jnp.concatenate(p_new[2 * h:2 * h + 4], axis=0)
        acc = jnp.dot(pn4, vn[:, h * vd:(h + 2) * vd], preferred_element_type=F32)
        for p in range(n_pages):
            v2 = jnp.concatenate([v_page(p, h), v_page(p, h + 1)], axis=1)
            acc = acc + jnp.dot(p_scr[h // 2, :, p * PAGE_SIZE:(p + 1) * PAGE_SIZE], v2,
                                preferred_element_type=F32)
        for a in range(2):
            blk = acc[2 * a * ts:2 * (a + 1) * ts, a * vd:(a + 1) * vd]
            o = _diff_combine(blk[:ts], row_sum[2 * (h + a)], blk[ts:], row_sum[2 * (h + a) + 1],
                              lam, subln, lam_init)
            o_ref[i, :, (h + a) * vd:(h + a + 1) * vd] = o.astype(BF16)


def _attn_kernel(bounded_ref, pt_ref, qT_ref, kT_ref, v_ref, lam_ref, subln_ref, qs_ref, kn_ref,
                 vn_ref, ck_hbm, cv_hbm, o_ref, os_ref, qs_scr, m_scr, acc_scr, s_scr, p_scr, kbuf,
                 vbuf, sem, *, layer, n_seq, n_pages, tq, tk, rc, lam_init):
    qi = pl.program_id(2)
    vd = v_ref.shape[2]
    n_heads = pl.num_programs(1)
    step = (pl.program_id(0) * n_heads + pl.program_id(1)) * pl.num_programs(2) + qi
    n_steps = pl.num_programs(0) * n_heads * pl.num_programs(2)
    slot = step % 2

    def page_copies(for_step, to_slot, lookup=True):
        copies = []
        for idx in range(n_seq * n_pages):
            page = pt_ref[for_step * (n_seq * n_pages) + idx] if lookup else 0
            copies.append(pltpu.make_async_copy(ck_hbm.at[layer, page], kbuf.at[to_slot, idx],
                                                sem.at[to_slot, 0]))
            copies.append(pltpu.make_async_copy(cv_hbm.at[layer, page], vbuf.at[to_slot, idx],
                                                sem.at[to_slot, 1]))
        return copies

    @pl.when(step == 0)
    def _():
        for c in page_copies(0, 0):
            c.start()

    @pl.when(step + 1 < n_steps)
    def _():
        for c in page_copies(step + 1, 1 - slot):
            c.start()

    for c in page_copies(step, slot, lookup=False):
        c.wait()

    lam = _lambda_full(lam_ref, lam_init)
    subln = subln_ref[...]

    q = qT_ref[0].T
    lane = lax.broadcasted_iota(jnp.int32, q.shape, 1)
    qs_scr[0:tq, :] = jnp.where(lane < HEAD_DIM, q, 0.0).astype(BF16)
    qs_scr[tq:2 * tq, :] = jnp.where(lane >= HEAD_DIM, q, 0.0).astype(BF16)
    acc_scr[...] = jnp.zeros(acc_scr.shape, F32)
    ones = jnp.ones((max(tk, tq), vd), BF16)
    lower_tri = (lax.broadcasted_iota(jnp.int32, (rc, rc), 1)
                 <= lax.broadcasted_iota(jnp.int32, (rc, rc), 0))

    def keys(k0, n):
        kT = kT_ref[0, :, pl.ds(k0, n)]
        vv = jnp.concatenate([v_ref[0, pl.ds(k0, n), :], ones[:n]], axis=1)
        return kT, vv

    def update(r, kT, vv, bounded, triangle=False):
        s = jnp.dot(qs_scr[r:r + rc, :], kT, preferred_element_type=F32)
        if triangle:
            n = s.shape[1]
            tri = jnp.where(lower_tri, s[:, n - rc:], -jnp.inf)
            s = tri if n == rc else jnp.concatenate([s[:, :n - rc], tri], axis=1)
        if bounded:
            pv = jnp.dot(jnp.exp2(s).astype(BF16), vv, preferred_element_type=F32)
            acc_scr[r:r + rc, :] += pv
        else:
            m_old = m_scr[r:r + rc, :]
            m_new = jnp.maximum(m_old, jnp.max(s, axis=1, keepdims=True))
            alpha = jnp.exp2(m_old - m_new)
            p = jnp.exp2(s - m_new)
            pv = jnp.dot(p.astype(BF16), vv, preferred_element_type=F32)
            acc_scr[r:r + rc, :] = acc_scr[r:r + rc, :] * alpha + pv
            m_scr[r:r + rc, :] = m_new

    def run(bounded):
        def body(jb, carry):
            kT, vv = keys(pl.multiple_of(jb * tk, tk), tk)
            for r in range(0, 2 * tq, rc):
                update(r, kT, vv, bounded)
            return carry

        lax.fori_loop(0, qi * (tq // tk), body, 0)
        for i in range(n_seq):
            k_page = lambda p, hc, i=i: kbuf[slot, i * n_pages + p, hc]
            v_page = lambda p, h, i=i: vbuf.at[slot, i * n_pages + p][
                pl.ds(h, PAGE_SIZE, stride=vbuf.shape[2] // PAGE_SIZE), :]
            _sample_attention(i, qs_ref, kn_ref, vn_ref, k_page, v_page, n_pages, os_ref, s_scr,
                              p_scr, lam, subln, lam_init)
        base = pl.multiple_of(qi * tq, tq)
        for r in range(0, 2 * tq, rc):
            kT, vv = keys(base, (r % tq) + rc)
            update(r, kT, vv, bounded, triangle=True)
        acc = acc_scr[...]
        o = _diff_combine(acc[0:tq, 0:vd], acc[0:tq, vd:vd + 1], acc[tq:, 0:vd],
                          acc[tq:, vd:vd + 1], lam, subln, lam_init)
        o_ref[0] = o.astype(BF16)

    is_bounded = bounded_ref[0] == 1

    @pl.when(is_bounded)
    def _():
        run(True)

    @pl.when(jnp.logical_not(is_bounded))
    def _():
        m_scr[...] = jnp.full(m_scr.shape, -jnp.inf, F32)
        run(False)


def _attention(bounded, page_table_flat, qT, kT16, v16, q_s, kn_s, vn_s, lam_params, subln_all,
               cache_kT, cache_v, j, lam_init, *, tq, tk, rc):
    b, d, t = qT.shape
    bs, ts, _ = q_s.shape
    vd = 2 * HEAD_DIM
    n_heads = d // vd
    nq = t // tq
    n_steps = b * n_heads * nq
    n_seq = bs // n_steps
    assert n_seq * n_steps == bs, "sample sequences must split evenly over the prompt grid steps"
    n_pages = page_table_flat.shape[0] // bs
    n_hc = cache_kT.shape[2]

    def step(bi, hi, qi):
        return (bi * n_heads + hi) * nq + qi

    seq_rows = pl.BlockSpec((n_seq, ts, d), lambda bi, hi, qi, fl, pt: (step(bi, hi, qi), 0, 0))
    sample_specs = [seq_rows] * 3 + [pl.BlockSpec(memory_space=pl.ANY)] * 2
    sample_args = [q_s, kn_s, vn_s, cache_kT, cache_v]
    grid_spec = pltpu.PrefetchScalarGridSpec(
        num_scalar_prefetch=2,
        grid=(b, n_heads, nq),
        in_specs=[
            pl.BlockSpec((1, vd, tq), lambda bi, hi, qi, fl, pt: (bi, hi, qi)),
            pl.BlockSpec((1, vd, t), lambda bi, hi, qi, fl, pt: (bi, hi, 0)),
            pl.BlockSpec((1, t, vd), lambda bi, hi, qi, fl, pt: (bi, 0, hi)),
            pl.BlockSpec((None, 4, HEAD_DIM), lambda bi, hi, qi, fl, pt: (j, 0, 0)),
            pl.BlockSpec((None, 1, vd), lambda bi, hi, qi, fl, pt: (j, 0, 0)),
        ] + sample_specs,
        out_specs=[pl.BlockSpec((1, tq, vd), lambda bi, hi, qi, fl, pt: (bi, qi, hi)), seq_rows],
        scratch_shapes=[
            pltpu.VMEM((2 * tq, vd), BF16),
            pltpu.VMEM((2 * tq, 1), F32),
            pltpu.VMEM((2 * tq, 2 * vd), F32),
            pltpu.VMEM((n_hc, ts, n_pages * PAGE_SIZE), F32),
            pltpu.VMEM((n_hc // 4, 4 * ts, n_pages * PAGE_SIZE), F32),
            pltpu.VMEM((2, n_seq * n_pages) + cache_kT.shape[2:], F32),
            pltpu.VMEM((2, n_seq * n_pages) + cache_v.shape[2:], F32),
            pltpu.SemaphoreType.DMA((2, 2)),
        ],
    )
    return pl.pallas_call(
        functools.partial(_attn_kernel, layer=j, n_seq=n_seq, n_pages=n_pages, tq=tq, tk=tk, rc=rc,
                          lam_init=lam_init),
        grid_spec=grid_spec,
        out_shape=[jax.ShapeDtypeStruct((b, t, d), BF16), jax.ShapeDtypeStruct((bs, ts, d), BF16)],
        compiler_params=_cparams("arbitrary", "arbitrary", "arbitrary"),
        name="attention",
    )(bounded, page_table_flat, qT, kT16, v16, lam_params, subln_all, *sample_args)


def _scores_bounded(q_gain, k_gain):
    bound = (math.sqrt(HEAD_DIM) * math.log2(math.e)
             * jnp.max(jnp.abs(q_gain), axis=-1) * jnp.max(jnp.abs(k_gain), axis=-1))
    return (bound <= SCORE_BOUND_LOG2).astype(jnp.int32)


def _rope_tables(pos):
    inv = ROPE_THETA ** (-jnp.arange(HALF, dtype=F32) / HALF)
    ang = inv[:, None] * pos.astype(F32)[None, :]
    return jnp.cos(ang), jnp.sin(ang)


def kernel(x_prompt, x_sample, cache_k, cache_v, page_table, state_pool, norm_mix, norm_ffn, w_pool,
           pool_scale, w_qkv, q_norm, k_norm, lambda_q1, lambda_k1, lambda_q2, lambda_k2, subln, w_o,
           w_up, w_down):
    bp, sp, d = x_prompt.shape
    bs, ts, _ = x_sample.shape
    depth = norm_mix.shape[0]
    past_len = page_table.shape[1] * PAGE_SIZE
    n_hc = d // HEAD_DIM
    n_heads = n_hc // 2
    vd = 2 * HEAD_DIM

    w_up16 = w_up.astype(BF16)
    w_down16 = w_down.astype(BF16)
    w_pool16 = w_pool.astype(BF16)
    w_o16 = w_o.astype(BF16)
    wqkT16 = jnp.swapaxes(w_qkv[:, :, :2 * d], 1, 2).astype(BF16)
    wv16 = w_qkv[:, :, 2 * d:].astype(BF16)
    gq = q_norm[:, :, None]
    gk = k_norm[:, :, None]
    lam_params = jnp.stack([lambda_q1, lambda_k1, lambda_q2, lambda_k2], axis=1)
    norm_mix = norm_mix[:, None, :]
    norm_ffn = norm_ffn[:, None, :]
    pool_scale = pool_scale[:, None, :]
    subln = subln[:, None, :]

    cos_p, sin_p = _rope_tables(jnp.arange(sp))
    cos_s, sin_s = _rope_tables(past_len + jnp.tile(jnp.arange(ts), bs))

    cache_kT = jnp.transpose(cache_k, (0, 1, 3, 4, 2))
    state_tm = jnp.transpose(state_pool, (0, 2, 1, 3))
    cache_v2 = cache_v.reshape(cache_v.shape[0], cache_v.shape[1], PAGE_SIZE * n_heads, vd)
    pt_flat = page_table.reshape(-1)
    bounded = _scores_bounded(q_norm, k_norm)

    n_attn = depth // N_MIXERS
    mlp = functools.partial(_ffn, tm=FFN_ROWS, tf=FFN_COLS)
    xp = x_prompt.reshape(bp * sp, d)
    xs = x_sample.reshape(bs * ts, d)
    kv_p = None
    nk_s, nv_s, np_p, np_s = [], [], [], []
    for i in range(depth):
        j = i // N_MIXERS
        if i % N_MIXERS == 0:
            xp, pool_p = mlp(xp, norm_ffn, w_up16, w_down16, i,
                             pool=(norm_mix, w_pool16, pool_scale, j, sp))
            np_p.append(pool_p.reshape(bp, POOL_HALO, d)[:, POOL_HALO - POOL_BUF:])
            xs_tm = jnp.swapaxes(xs.reshape(bs, ts, d), 0, 1)
            xs_tm, pool_s = _pool_sample(xs_tm, state_tm, norm_mix, w_pool16, pool_scale, i, j,
                                         pos0=past_len, bb=32)
            xs = jnp.swapaxes(xs_tm, 0, 1).reshape(bs * ts, d)
            np_s.append(pool_s)
            xs = mlp(xs, norm_ffn, w_up16, w_down16, i)
        else:
            lam_init = 0.8 - 0.6 * math.exp(-0.3 * i)
            qT, kT32, kT16, v32, v16 = _qkv(xp.reshape(bp, sp, d), norm_mix, wqkT16, wv16, gq, gk,
                                            cos_p, sin_p, i, j, lambda bi, ti: (0, ti), tm=512,
                                            n_slots=n_attn, slot=j, prev=kv_p)
            kv_p = (kT32, v32)
            qT_s, kT32_s, _, v32_s, _ = _qkv(xs[None], norm_mix, wqkT16, wv16, gq, gk, cos_s, sin_s,
                                             i, j, lambda bi, ti: (0, ti), tm=512)
            q_s = qT_s[0].T.reshape(bs, ts, d)
            kn_s = kT32_s[0, 0].T.reshape(bs, ts, d)
            vn_s = v32_s.reshape(bs, ts, d)
            o_p, o_s = _attention(bounded[j:j + 1], pt_flat, qT, kT16, v16, q_s, kn_s, vn_s,
                                  lam_params, subln, cache_kT, cache_v2, j, lam_init,
                                  tq=1024, tk=1024, rc=256)
            xp = mlp(xp, norm_ffn, w_up16, w_down16, i, proj=(o_p.reshape(bp * sp, d), w_o16, j))
            xs = mlp(xs, norm_ffn, w_up16, w_down16, i, proj=(o_s.reshape(bs * ts, d), w_o16, j))
            nk_s.append(kn_s.reshape(bs, ts, n_hc, HEAD_DIM))
            nv_s.append(vn_s.reshape(bs, ts, n_heads, vd))

    kT32, v32 = kv_p
    new_k_prompt = jnp.transpose(kT32.reshape(n_attn, bp, n_hc, HEAD_DIM, sp), (0, 1, 4, 2, 3))
    new_v_prompt = v32.reshape(n_attn, bp, sp, n_heads, vd)
    new_k_sample = jnp.stack(nk_s)
    new_v_sample = jnp.stack(nv_s)
    new_pool_prompt = jnp.stack(np_p)
    new_pool_sample = jnp.transpose(jnp.stack(np_s), (0, 2, 1, 3))
    return (xp.reshape(bp, sp, d), xs.reshape(bs, ts, d), new_k_prompt, new_v_prompt, new_k_sample,
            new_v_sample, new_pool_prompt, new_pool_sample)
```

```python
import functools
import math

import jax
import jax.numpy as jnp
from jax import lax
from jax.experimental import pallas as pl
from jax.experimental.pallas import tpu as pltpu

F32 = jnp.float32
BF16 = jnp.bfloat16

N_MIXERS = 2
POOL_WINDOWS = (2, 4, 8, 16)
POOL_BUF = max(POOL_WINDOWS) - 1
POOL_HALO = 16
HEAD_DIM = 64
HALF = HEAD_DIM // 2
ROPE_THETA = 10000.0
EPS = 1e-6
SUBLN_EPS = 1e-5
PAGE_SIZE = 128
SCORE_BOUND_LOG2 = 48.0
Q_SCALE = math.log2(math.e) * HEAD_DIM ** -0.5
FFN_ROWS = 1024
FFN_COLS = 1024

V7X_VMEM_LIMIT_BYTES = 56 * 1024 * 1024


def _cparams(*sem):
    return pltpu.CompilerParams(dimension_semantics=sem, vmem_limit_bytes=V7X_VMEM_LIMIT_BYTES)


def _rms(x, g, eps=EPS):
    ms = jnp.mean(x * x, axis=-1, keepdims=True)
    return x * lax.rsqrt(ms + eps) * g


def _mlp_chunk(h_scr, wu_ref, wd_ref, o_ref):
    u = jnp.dot(h_scr[...], wu_ref[...], preferred_element_type=F32)
    a = jnp.maximum(u, 0.0)
    a = (a * a).astype(BF16)
    o_ref[...] += jnp.dot(a, wd_ref[...], preferred_element_type=F32)


def _ffn_kernel(x_ref, g_ref, wu_ref, wd_ref, o_ref, h_scr):
    @pl.when(pl.program_id(1) == 0)
    def _():
        x = x_ref[...]
        h_scr[...] = _rms(x, g_ref[...]).astype(BF16)
        o_ref[...] = x

    _mlp_chunk(h_scr, wu_ref, wd_ref, o_ref)


def _proj_ffn_kernel(x_ref, a_ref, wo_ref, g_ref, wu_ref, wd_ref, o_ref, h_scr):
    @pl.when(pl.program_id(1) == 0)
    def _():
        x = x_ref[...] + jnp.dot(a_ref[...], wo_ref[...], preferred_element_type=F32)
        h_scr[...] = _rms(x, g_ref[...]).astype(BF16)
        o_ref[...] = x

    _mlp_chunk(h_scr, wu_ref, wd_ref, o_ref)


def _pool_ffn_kernel(x_ref, prev_ref, gm_ref, wp_ref, sc_ref, g_ref, wu_ref, wd_ref, o_ref, np_ref,
                     h_scr, ext_scr, *, tm, tiles_per_seq):
    @pl.when(pl.program_id(1) == 0)
    def _():
        ti = pl.program_id(0) % tiles_per_seq
        gm = gm_ref[...]
        x = x_ref[...]
        h = _rms(x, gm)
        hp = _rms(prev_ref[...], gm)
        hp = jnp.where(ti == 0, 0.0, hp)
        ext_scr[0:POOL_HALO, :] = hp
        ext_scr[POOL_HALO:POOL_HALO + tm, :] = h
        pos1 = ti * tm + lax.broadcasted_iota(jnp.int32, (tm, 1), 0) + 1
        gw = x.shape[1] // len(POOL_WINDOWS)
        for gi, w in enumerate(POOL_WINDOWS):
            lo = gi * gw
            hg = h[:, lo:lo + gw]
            s = hg
            for j in range(1, w):
                s = s + ext_scr[POOL_HALO - j:POOL_HALO - j + tm, lo:lo + gw]
            cnt = jnp.minimum(pos1, w).astype(F32)
            dlt = (s / cnt - hg).astype(BF16)
            y = jnp.dot(dlt, wp_ref[gi], preferred_element_type=F32)
            o_ref[:, lo:lo + gw] = x[:, lo:lo + gw] + y * sc_ref[:, lo:lo + gw]
        h_scr[...] = _rms(o_ref[...], g_ref[...]).astype(BF16)

        @pl.when(ti == tiles_per_seq - 1)
        def _():
            np_ref[...] = ext_scr[tm:tm + POOL_HALO, :]

    _mlp_chunk(h_scr, wu_ref, wd_ref, o_ref)


def _ffn(x2d, g, w_up, w_down, layer, *, tm, tf, proj=None, pool=None):
    n, d = x2d.shape
    dff = w_up.shape[-1]
    row = pl.BlockSpec((tm, d), lambda i, c: (i, 0))
    mlp_specs = [
        pl.BlockSpec((None, 1, d), lambda i, c: (layer, 0, 0)),
        pl.BlockSpec((None, d, tf), lambda i, c: (layer, 0, c)),
        pl.BlockSpec((None, tf, d), lambda i, c: (layer, c, 0)),
    ]
    mlp_args = [g, w_up, w_down]
    out_specs, out_shape = row, jax.ShapeDtypeStruct((n, d), F32)
    scratch = [pltpu.VMEM((tm, d), BF16)]
    if proj is not None:
        a2d, w_o, j = proj
        body, name = _proj_ffn_kernel, "proj_ffn"
        in_specs = [row, row, pl.BlockSpec((None, d, d), lambda i, c: (j, 0, 0))] + mlp_specs
        args = [x2d, a2d, w_o] + mlp_args
    elif pool is not None:
        gm, w_pool, scale, j, seq_len = pool
        tiles_per_seq = seq_len // tm
        halo_blocks = tm // POOL_HALO
        n_groups = len(POOL_WINDOWS)
        gw = d // n_groups
        body = functools.partial(_pool_ffn_kernel, tm=tm, tiles_per_seq=tiles_per_seq)
        name = "pool_ffn"
        in_specs = [
            row,
            pl.BlockSpec((POOL_HALO, d), lambda i, c: (jnp.maximum(i * halo_blocks - 1, 0), 0)),
            pl.BlockSpec((None, 1, d), lambda i, c: (layer, 0, 0)),
            pl.BlockSpec((None, n_groups, gw, gw), lambda i, c: (j, 0, 0, 0)),
            pl.BlockSpec((None, 1, d), lambda i, c: (j, 0, 0)),
        ] + mlp_specs
        args = [x2d, x2d, gm, w_pool, scale] + mlp_args
        out_specs = [row, pl.BlockSpec((POOL_HALO, d), lambda i, c: (i // tiles_per_seq, 0))]
        out_shape = [out_shape,
                     jax.ShapeDtypeStruct((n // seq_len * POOL_HALO, d), F32)]
        scratch = scratch + [pltpu.VMEM((POOL_HALO + tm, d), F32)]
    else:
        body, name = _ffn_kernel, "ffn"
        in_specs = [row] + mlp_specs
        args = [x2d] + mlp_args
    return pl.pallas_call(
        body,
        grid=(n // tm, dff // tf),
        in_specs=in_specs,
        out_specs=out_specs,
        out_shape=out_shape,
        scratch_shapes=scratch,
        compiler_params=_cparams("parallel", "arbitrary"),
        name=name,
    )(*args)


def _pool_sample_kernel(x_ref, st_ref, g_ref, wp_ref, sc_ref, o_ref, np_ref, *, pos0):
    ts = x_ref.shape[0]
    g = g_ref[...]
    xs = [x_ref[t] for t in range(ts)]
    ext = [st_ref[r] for r in range(POOL_BUF)] + [_rms(xt, g) for xt in xs]
    gw = xs[0].shape[1] // len(POOL_WINDOWS)
    for t in range(ts):
        e = POOL_BUF + t
        for gi, w in enumerate(POOL_WINDOWS):
            lo = gi * gw
            s = ext[e][:, lo:lo + gw]
            for jj in range(1, w):
                s = s + ext[e - jj][:, lo:lo + gw]
            cnt = float(min(pos0 + t + 1, w))
            dlt = (s / cnt - ext[e][:, lo:lo + gw]).astype(BF16)
            y = jnp.dot(dlt, wp_ref[gi], preferred_element_type=F32)
            o_ref[t, :, lo:lo + gw] = xs[t][:, lo:lo + gw] + y * sc_ref[:, lo:lo + gw]
    for r in range(POOL_BUF):
        np_ref[r] = ext[ts + r]


def _pool_sample(x_tm, state_tm, g_all, w_pool, scale_all, layer, j, *, pos0, bb):
    ts, b, d = x_tm.shape
    n_groups = len(POOL_WINDOWS)
    gw = d // n_groups
    return pl.pallas_call(
        functools.partial(_pool_sample_kernel, pos0=pos0),
        grid=(b // bb,),
        in_specs=[
            pl.BlockSpec((ts, bb, d), lambda i: (0, i, 0)),
            pl.BlockSpec((None, POOL_BUF, bb, d), lambda i: (j, 0, i, 0)),
            pl.BlockSpec((None, 1, d), lambda i: (layer, 0, 0)),
            pl.BlockSpec((None, n_groups, gw, gw), lambda i: (j, 0, 0, 0)),
            pl.BlockSpec((None, 1, d), lambda i: (j, 0, 0)),
        ],
        out_specs=[
            pl.BlockSpec((ts, bb, d), lambda i: (0, i, 0)),
            pl.BlockSpec((POOL_BUF, bb, d), lambda i: (0, i, 0)),
        ],
        out_shape=[
            jax.ShapeDtypeStruct((ts, b, d), F32),
            jax.ShapeDtypeStruct((POOL_BUF, b, d), F32),
        ],
        compiler_params=_cparams("parallel"),
        name="pool_sample",
    )(x_tm, state_tm, g_all, w_pool, scale_all)


def _qkv_kernel(x_ref, g_ref, wqk_ref, wv_ref, gq_ref, gk_ref, cos_ref, sin_ref, *rest, q_scale,
                write_slot):
    q_ref, kT32_ref, kT16_ref, v32_ref, v16_ref = rest[-5:]
    for w in range(kT32_ref.shape[0]):
        if w != write_slot:
            kT32_ref[w] = jnp.zeros(kT32_ref.shape[1:], F32)
            v32_ref[w] = jnp.zeros(v32_ref.shape[1:], F32)
    x = x_ref[0]
    tm, d = x.shape
    h = _rms(x, g_ref[...]).astype(BF16)
    v = jnp.dot(h, wv_ref[...], preferred_element_type=F32)
    v32_ref[write_slot, 0] = v
    v16_ref[0] = v.astype(BF16)
    cos = cos_ref[...][None]
    sin = sin_ref[...][None]
    rows = 4 * HEAD_DIM
    for part in range(2):
        gn = (gq_ref if part == 0 else gk_ref)[...][None]
        for c in range(d // rows):
            r0 = part * d + c * rows
            yT = lax.dot_general(wqk_ref[r0:r0 + rows, :], h, (((1,), (1,)), ((), ())),
                                 preferred_element_type=F32)
            y3 = yT.reshape(rows // HEAD_DIM, HEAD_DIM, tm)
            ms = jnp.mean(y3 * y3, axis=1, keepdims=True)
            yn = y3 * lax.rsqrt(ms + EPS) * gn
            x1 = yn[:, :HALF, :]
            x2 = yn[:, HALF:, :]
            out = jnp.concatenate([x1 * cos - x2 * sin, x2 * cos + x1 * sin], axis=1)
            out = out.reshape(rows, tm)
            if part == 0:
                q_ref[0, :, c * rows:(c + 1) * rows] = (out * q_scale).T.astype(BF16)
            else:
                kT32_ref[write_slot, 0, c * rows:(c + 1) * rows, :] = out
                kT16_ref[0, c * rows:(c + 1) * rows, :] = out.astype(BF16)


def _qkv(x, g_all, wqkT, wv, gq, gk, cosT, sinT, layer, j, tab_map, *, tm, n_slots=1, slot=0,
         prev=None):
    b, t, d = x.shape
    tok = lambda bi, i: (bi, i, 0)
    tr = lambda bi, i: (bi, 0, i)
    in_specs = [
        pl.BlockSpec((1, tm, d), tok),
        pl.BlockSpec((None, 1, d), lambda bi, i: (layer, 0, 0)),
        pl.BlockSpec((None, 2 * d, d), lambda bi, i: (j, 0, 0)),
        pl.BlockSpec((None, d, d), lambda bi, i: (j, 0, 0)),
        pl.BlockSpec((None, HEAD_DIM, 1), lambda bi, i: (j, 0, 0)),
        pl.BlockSpec((None, HEAD_DIM, 1), lambda bi, i: (j, 0, 0)),
        pl.BlockSpec((HALF, tm), tab_map),
        pl.BlockSpec((HALF, tm), tab_map),
    ]
    args = [x, g_all, wqkT, wv, gq, gk, cosT, sinT]
    aliases = {}
    n_blk, blk, write_slot = n_slots, 0, slot
    if prev is not None:
        aliases = {len(args): 1, len(args) + 1: 3}
        in_specs = in_specs + [pl.BlockSpec(memory_space=pl.ANY)] * 2
        args = args + list(prev)
        n_blk, blk, write_slot = 1, slot, 0
    return pl.pallas_call(
        functools.partial(_qkv_kernel, q_scale=Q_SCALE, write_slot=write_slot),
        grid=(b, t // tm),
        in_specs=in_specs,
        out_specs=[
            pl.BlockSpec((1, tm, d), tok),
            pl.BlockSpec((n_blk, 1, d, tm), lambda bi, i: (blk, bi, 0, i)),
            pl.BlockSpec((1, d, tm), tr),
            pl.BlockSpec((n_blk, 1, tm, d), lambda bi, i: (blk, bi, i, 0)),
            pl.BlockSpec((1, tm, d), tok),
        ],
        out_shape=[
            jax.ShapeDtypeStruct((b, t, d), BF16),
            jax.ShapeDtypeStruct((n_slots, b, d, t), F32),
            jax.ShapeDtypeStruct((b, d, t), BF16),
            jax.ShapeDtypeStruct((n_slots, b, t, d), F32),
            jax.ShapeDtypeStruct((b, t, d), BF16),
        ],
        input_output_aliases=aliases,
        compiler_params=_cparams("parallel", "parallel"),
        name="qkv",
    )(*args)


def _lambda_full(lam_ref, lam_init):
    lp = lam_ref[...]
    a = jnp.sum(lp[0:1] * lp[1:2], axis=-1, keepdims=True)
    b = jnp.sum(lp[2:3] * lp[3:4], axis=-1, keepdims=True)
    return jnp.exp(a) - jnp.exp(b) + lam_init


def _diff_combine(o1, l1, o2, l2, lam, subln, lam_init):
    o = o1 / l1 - lam * (o2 / l2)
    return _rms(o, subln, SUBLN_EPS) * (1.0 - lam_init)


def _sample_attention(i, q_ref, kn_ref, vn_ref, k_page, v_page, n_pages, o_ref, s_scr, p_scr, lam,
                      subln, lam_init, bounded):
    q = q_ref[i].astype(F32)
    kn = kn_ref[i]
    vn = vn_ref[i]
    ts = q.shape[0]
    vd = 2 * HEAD_DIM
    n_heads = q.shape[1] // vd
    n_hc = 2 * n_heads
    causal = (lax.broadcasted_iota(jnp.int32, (ts, ts), 1)
              <= lax.broadcasted_iota(jnp.int32, (ts, ts), 0))
    nt = (((1,), (1,)), ((), ()))
    qh = [q[:, hc * HEAD_DIM:(hc + 1) * HEAD_DIM] for hc in range(n_hc)]
    for hc in range(n_hc):
        for p in range(0, n_pages, 2):
            k2 = jnp.concatenate([k_page(p, hc), k_page(p + 1, hc)], axis=1)
            s_scr[hc, :, p * PAGE_SIZE:(p + 2) * PAGE_SIZE] = jnp.dot(
                qh[hc], k2, preferred_element_type=F32)
    p_new, row_sum = [], []
    for hc in range(n_hc):
        s = s_scr[hc]
        s_new = lax.dot_general(qh[hc], kn[:, hc * HEAD_DIM:(hc + 1) * HEAD_DIM], nt,
                                preferred_element_type=F32)
        s_new = jnp.where(causal, s_new, -jnp.inf)
        if bounded:
            pr = jnp.exp2(s)
            pn = jnp.exp2(s_new)
        else:
            m = jnp.maximum(jnp.max(s, axis=1, keepdims=True),
                            jnp.max(s_new, axis=1, keepdims=True))
            pr = jnp.exp2(s - m)
            pn = jnp.exp2(s_new - m)
        p_scr[hc // 4, (hc % 4) * ts:(hc % 4 + 1) * ts, :] = pr
        p_new.append(pn)
        row_sum.append(jnp.sum(pr, axis=1, keepdims=True) + jnp.sum(pn, axis=1, keepdims=True))
    for h in range(0, n_heads, 2):
        pn4 = jnp.concatenate(p_new[2 * h:2 * h + 4], axis=0)
        acc = jnp.dot(pn4, vn[:, h * vd:(h + 2) * vd], preferred_element_type=F32)
        for p in range(n_pages):
            v2 = jnp.concatenate([v_page(p, h), v_page(p, h + 1)], axis=1)
            acc = acc + jnp.dot(p_scr[h // 2, :, p * PAGE_SIZE:(p + 1) * PAGE_SIZE], v2,
                                preferred_element_type=F32)
        for a in range(2):
            blk = acc[2 * a * ts:2 * (a + 1) * ts, a * vd:(a + 1) * vd]
            o = _diff_combine(blk[:ts], row_sum[2 * (h + a)], blk[ts:], row_sum[2 * (h + a) + 1],
                              lam, subln, lam_init)
            o_ref[i, :, (h + a) * vd:(h + a + 1) * vd] = o.astype(BF16)


def _attn_kernel(bounded_ref, pt_ref, qp_ref, kT_ref, v_ref, lam_ref, subln_ref, qs_ref, kn_ref,
                 vn_ref, ck_hbm, cv_hbm, o_ref, os_ref, qs_scr, m_scr, acc_scr, s_scr, p_scr, kbuf,
                 vbuf, sem, *, layer, n_seq, n_pages, tq, tk, rc, lam_init):
    qi = pl.program_id(2)
    vd = v_ref.shape[2]
    n_heads = pl.num_programs(1)
    step = (pl.program_id(0) * n_heads + pl.program_id(1)) * pl.num_programs(2) + qi
    n_steps = pl.num_programs(0) * n_heads * pl.num_programs(2)
    slot = step % 2

    def page_copies(for_step, to_slot, lookup=True):
        copies = []
        for idx in range(n_seq * n_pages):
            page = pt_ref[for_step * (n_seq * n_pages) + idx] if lookup else 0
            copies.append(pltpu.make_async_copy(ck_hbm.at[layer, page], kbuf.at[to_slot, idx],
                                                sem.at[to_slot, 0]))
            copies.append(pltpu.make_async_copy(cv_hbm.at[layer, page], vbuf.at[to_slot, idx],
                                                sem.at[to_slot, 1]))
        return copies

    @pl.when(step == 0)
    def _():
        for c in page_copies(0, 0):
            c.start()

    @pl.when(step + 1 < n_steps)
    def _():
        for c in page_copies(step + 1, 1 - slot):
            c.start()

    for c in page_copies(step, slot, lookup=False):
        c.wait()

    lam = _lambda_full(lam_ref, lam_init)
    subln = subln_ref[...]

    q = qp_ref[0]
    lane = lax.broadcasted_iota(jnp.int32, q.shape, 1)
    qs_scr[0:tq, :] = jnp.where(lane < HEAD_DIM, q, jnp.zeros_like(q))
    qs_scr[tq:2 * tq, :] = jnp.where(lane >= HEAD_DIM, q, jnp.zeros_like(q))
    acc_scr[...] = jnp.zeros(acc_scr.shape, F32)
    ones = jnp.ones((max(tk, tq), vd), BF16)
    lower_tri = (lax.broadcasted_iota(jnp.int32, (rc, rc), 1)
                 <= lax.broadcasted_iota(jnp.int32, (rc, rc), 0))

    def keys(k0, n):
        kT = kT_ref[0, :, pl.ds(k0, n)]
        vv = jnp.concatenate([v_ref[0, pl.ds(k0, n), :], ones[:n]], axis=1)
        return kT, vv

    def update(r, kT, vv, bounded, triangle=False):
        s = jnp.dot(qs_scr[r:r + rc, :], kT, preferred_element_type=F32)
        if triangle:
            n = s.shape[1]
            tri = jnp.where(lower_tri, s[:, n - rc:], -jnp.inf)
            s = tri if n == rc else jnp.concatenate([s[:, :n - rc], tri], axis=1)
        if bounded:
            pv = jnp.dot(jnp.exp2(s).astype(BF16), vv, preferred_element_type=F32)
            acc_scr[r:r + rc, :] += pv
        else:
            m_old = m_scr[r:r + rc, :]
            m_new = jnp.maximum(m_old, jnp.max(s, axis=1, keepdims=True))
            alpha = jnp.exp2(m_old - m_new)
            p = jnp.exp2(s - m_new)
            pv = jnp.dot(p.astype(BF16), vv, preferred_element_type=F32)
            acc_scr[r:r + rc, :] = acc_scr[r:r + rc, :] * alpha + pv
            m_scr[r:r + rc, :] = m_new

    def run(bounded):
        def body(jb, carry):
            kT, vv = keys(pl.multiple_of(jb * tk, tk), tk)
            for r in range(0, 2 * tq, rc):
                update(r, kT, vv, bounded)
            return carry

        lax.fori_loop(0, qi * (tq // tk), body, 0)
        for i in range(n_seq):
            k_page = lambda p, hc, i=i: kbuf[slot, i * n_pages + p, hc]
            v_page = lambda p, h, i=i: vbuf.at[slot, i * n_pages + p][
                pl.ds(h, PAGE_SIZE, stride=vbuf.shape[2] // PAGE_SIZE), :]
            _sample_attention(i, qs_ref, kn_ref, vn_ref, k_page, v_page, n_pages, os_ref, s_scr,
                              p_scr, lam, subln, lam_init, bounded)
        base = pl.multiple_of(qi * tq, tq)
        for r in range(0, 2 * tq, rc):
            kT, vv = keys(base, (r % tq) + rc)
            update(r, kT, vv, bounded, triangle=True)
        acc = acc_scr[...]
        o = _diff_combine(acc[0:tq, 0:vd], acc[0:tq, vd:vd + 1], acc[tq:, 0:vd],
                          acc[tq:, vd:vd + 1], lam, subln, lam_init)
        o_ref[0] = o.astype(BF16)

    is_bounded = bounded_ref[0] == 1

    @pl.when(is_bounded)
    def _():
        run(True)

    @pl.when(jnp.logical_not(is_bounded))
    def _():
        m_scr[...] = jnp.full(m_scr.shape, -jnp.inf, F32)
        run(False)


def _attention(bounded, page_table_flat, q_p, kT16, v16, q_s, kn_s, vn_s, lam_params, subln_all,
               cache_kT, cache_v, j, lam_init, *, tq, tk, rc):
    b, t, d = q_p.shape
    bs, ts, _ = q_s.shape
    vd = 2 * HEAD_DIM
    n_heads = d // vd
    nq = t // tq
    n_steps = b * n_heads * nq
    n_seq = bs // n_steps
    assert n_seq * n_steps == bs, "sample sequences must split evenly over the prompt grid steps"
    n_pages = page_table_flat.shape[0] // bs
    n_hc = cache_kT.shape[2]

    def step(bi, hi, qi):
        return (bi * n_heads + hi) * nq + qi

    seq_rows = pl.BlockSpec((n_seq, ts, d), lambda bi, hi, qi, fl, pt: (step(bi, hi, qi), 0, 0))
    sample_specs = [seq_rows] * 3 + [pl.BlockSpec(memory_space=pl.ANY)] * 2
    sample_args = [q_s, kn_s, vn_s, cache_kT, cache_v]
    grid_spec = pltpu.PrefetchScalarGridSpec(
        num_scalar_prefetch=2,
        grid=(b, n_heads, nq),
        in_specs=[
            pl.BlockSpec((1, tq, vd), lambda bi, hi, qi, fl, pt: (bi, qi, hi)),
            pl.BlockSpec((1, vd, t), lambda bi, hi, qi, fl, pt: (bi, hi, 0)),
            pl.BlockSpec((1, t, vd), lambda bi, hi, qi, fl, pt: (bi, 0, hi)),
            pl.BlockSpec((None, 4, HEAD_DIM), lambda bi, hi, qi, fl, pt: (j, 0, 0)),
            pl.BlockSpec((None, 1, vd), lambda bi, hi, qi, fl, pt: (j, 0, 0)),
        ] + sample_specs,
        out_specs=[pl.BlockSpec((1, tq, vd), lambda bi, hi, qi, fl, pt: (bi, qi, hi)), seq_rows],
        scratch_shapes=[
            pltpu.VMEM((2 * tq, vd), BF16),
            pltpu.VMEM((2 * tq, 1), F32),
            pltpu.VMEM((2 * tq, 2 * vd), F32),
            pltpu.VMEM((n_hc, ts, n_pages * PAGE_SIZE), F32),
            pltpu.VMEM((n_hc // 4, 4 * ts, n_pages * PAGE_SIZE), F32),
            pltpu.VMEM((2, n_seq * n_pages) + cache_kT.shape[2:], F32),
            pltpu.VMEM((2, n_seq * n_pages) + cache_v.shape[2:], F32),
            pltpu.SemaphoreType.DMA((2, 2)),
        ],
    )
    return pl.pallas_call(
        functools.partial(_attn_kernel, layer=j, n_seq=n_seq, n_pages=n_pages, tq=tq, tk=tk, rc=rc,
                          lam_init=lam_init),
        grid_spec=grid_spec,
        out_shape=[jax.ShapeDtypeStruct((b, t, d), BF16), jax.ShapeDtypeStruct((bs, ts, d), BF16)],
        compiler_params=_cparams("arbitrary", "arbitrary", "arbitrary"),
        name="attention",
    )(bounded, page_table_flat, q_p, kT16, v16, lam_params, subln_all, *sample_args)


def _scores_bounded(q_gain, k_gain):
    bound = (math.sqrt(HEAD_DIM) * math.log2(math.e)
             * jnp.max(jnp.abs(q_gain), axis=-1) * jnp.max(jnp.abs(k_gain), axis=-1))
    return (bound <= SCORE_BOUND_LOG2).astype(jnp.int32)


def _rope_tables(pos):
    inv = ROPE_THETA ** (-jnp.arange(HALF, dtype=F32) / HALF)
    ang = inv[:, None] * pos.astype(F32)[None, :]
    return jnp.cos(ang), jnp.sin(ang)


def kernel(x_prompt, x_sample, cache_k, cache_v, page_table, state_pool, norm_mix, norm_ffn, w_pool,
           pool_scale, w_qkv, q_norm, k_norm, lambda_q1, lambda_k1, lambda_q2, lambda_k2, subln, w_o,
           w_up, w_down):
    bp, sp, d = x_prompt.shape
    bs, ts, _ = x_sample.shape
    depth = norm_mix.shape[0]
    past_len = page_table.shape[1] * PAGE_SIZE
    n_hc = d // HEAD_DIM
    n_heads = n_hc // 2
    vd = 2 * HEAD_DIM

    w_up16 = w_up.astype(BF16)
    w_down16 = w_down.astype(BF16)
    w_pool16 = w_pool.astype(BF16)
    w_o16 = w_o.astype(BF16)
    wqkT16 = jnp.swapaxes(w_qkv[:, :, :2 * d], 1, 2).astype(BF16)
    wv16 = w_qkv[:, :, 2 * d:].astype(BF16)
    gq = q_norm[:, :, None]
    gk = k_norm[:, :, None]
    lam_params = jnp.stack([lambda_q1, lambda_k1, lambda_q2, lambda_k2], axis=1)
    norm_mix = norm_mix[:, None, :]
    norm_ffn = norm_ffn[:, None, :]
    pool_scale = pool_scale[:, None, :]
    subln = subln[:, None, :]

    cos_p, sin_p = _rope_tables(jnp.arange(sp))
    cos_s, sin_s = _rope_tables(past_len + jnp.tile(jnp.arange(ts), bs))

    cache_kT = jnp.transpose(cache_k, (0, 1, 3, 4, 2))
    state_tm = jnp.transpose(state_pool, (0, 2, 1, 3))
    cache_v2 = cache_v.reshape(cache_v.shape[0], cache_v.shape[1], PAGE_SIZE * n_heads, vd)
    pt_flat = page_table.reshape(-1)
    bounded = _scores_bounded(q_norm, k_norm)

    n_attn = depth // N_MIXERS
    mlp = functools.partial(_ffn, tm=FFN_ROWS, tf=FFN_COLS)
    xp = x_prompt.reshape(bp * sp, d)
    xs = x_sample.reshape(bs * ts, d)
    kv_p = None
    nk_s, nv_s, np_p, np_s = [], [], [], []
    for i in range(depth):
        j = i // N_MIXERS
        if i % N_MIXERS == 0:
            xp, pool_p = mlp(xp, norm_ffn, w_up16, w_down16, i,
                             pool=(norm_mix, w_pool16, pool_scale, j, sp))
            np_p.append(pool_p.reshape(bp, POOL_HALO, d)[:, POOL_HALO - POOL_BUF:])
            xs_tm = jnp.swapaxes(xs.reshape(bs, ts, d), 0, 1)
            xs_tm, pool_s = _pool_sample(xs_tm, state_tm, norm_mix, w_pool16, pool_scale, i, j,
                                         pos0=past_len, bb=32)
            xs = jnp.swapaxes(xs_tm, 0, 1).reshape(bs * ts, d)
            np_s.append(pool_s)
            xs = mlp(xs, norm_ffn, w_up16, w_down16, i)
        else:
            lam_init = 0.8 - 0.6 * math.exp(-0.3 * i)
            q_p, kT32, kT16, v32, v16 = _qkv(xp.reshape(bp, sp, d), norm_mix, wqkT16, wv16, gq, gk,
                                             cos_p, sin_p, i, j, lambda bi, ti: (0, ti), tm=512,
                                             n_slots=n_attn, slot=j, prev=kv_p)
            kv_p = (kT32, v32)
            q_s, kT32_s, _, v32_s, _ = _qkv(xs[None], norm_mix, wqkT16, wv16, gq, gk, cos_s, sin_s,
                                            i, j, lambda bi, ti: (0, ti), tm=512)
            q_s = q_s.reshape(bs, ts, d)
            kn_s = kT32_s[0, 0].T.reshape(bs, ts, d)
            vn_s = v32_s.reshape(bs, ts, d)
            o_p, o_s = _attention(bounded[j:j + 1], pt_flat, q_p, kT16, v16, q_s, kn_s, vn_s,
                                  lam_params, subln, cache_kT, cache_v2, j, lam_init,
                                  tq=1024, tk=1024, rc=256)
            xp = mlp(xp, norm_ffn, w_up16, w_down16, i, proj=(o_p.reshape(bp * sp, d), w_o16, j))
            xs = mlp(xs, norm_ffn, w_up16, w_down16, i, proj=(o_s.reshape(bs * ts, d), w_o16, j))
            nk_s.append(kn_s.reshape(bs, ts, n_hc, HEAD_DIM))
            nv_s.append(vn_s.reshape(bs, ts, n_heads, vd))

    kT32, v32 = kv_p
    new_k_prompt = jnp.transpose(kT32.reshape(n_attn, bp, n_hc, HEAD_DIM, sp), (0, 1, 4, 2, 3))
    new_v_prompt = v32.reshape(n_attn, bp, sp, n_heads, vd)
    new_k_sample = jnp.stack(nk_s)
    new_v_sample = jnp.stack(nv_s)
    new_pool_prompt = jnp.stack(np_p)
    new_pool_sample = jnp.transpose(jnp.stack(np_s), (0, 2, 1, 3))
    return (xp.reshape(bp, sp, d), xs.reshape(bs, ts, d), new_k_prompt, new_v_prompt, new_k_sample,
            new_v_sample, new_pool_prompt, new_pool_sample)
```

```python
import functools
import math

import jax
import jax.numpy as jnp
from jax import lax
from jax.experimental import pallas as pl
from jax.experimental.pallas import tpu as pltpu

F32 = jnp.float32
BF16 = jnp.bfloat16

N_MIXERS = 2
POOL_WINDOWS = (2, 4, 8, 16)
POOL_BUF = max(POOL_WINDOWS) - 1
POOL_HALO = 16
HEAD_DIM = 64
HALF = HEAD_DIM // 2
ROPE_THETA = 10000.0
EPS = 1e-6
SUBLN_EPS = 1e-5
PAGE_SIZE = 128
SCORE_BOUND_LOG2 = 48.0
Q_SCALE = math.log2(math.e) * HEAD_DIM ** -0.5
FFN_ROWS = 1024
FFN_COLS = 1024

V7X_VMEM_LIMIT_BYTES = 56 * 1024 * 1024


def _cparams(*sem):
    return pltpu.CompilerParams(dimension_semantics=sem, vmem_limit_bytes=V7X_VMEM_LIMIT_BYTES)


def _rms(x, g, eps=EPS):
    ms = jnp.mean(x * x, axis=-1, keepdims=True)
    return x * lax.rsqrt(ms + eps) * g


def _mlp_chunk(h_scr, wu_ref, wd_ref, o_ref):
    u = jnp.dot(h_scr[...], wu_ref[...], preferred_element_type=F32)
    a = jnp.maximum(u, 0.0)
    a = (a * a).astype(BF16)
    o_ref[...] += jnp.dot(a, wd_ref[...], preferred_element_type=F32)


def _ffn_kernel(x_ref, g_ref, wu_ref, wd_ref, o_ref, h_scr):
    @pl.when(pl.program_id(1) == 0)
    def _():
        x = x_ref[...]
        h_scr[...] = _rms(x, g_ref[...]).astype(BF16)
        o_ref[...] = x

    _mlp_chunk(h_scr, wu_ref, wd_ref, o_ref)


def _proj_ffn_kernel(x_ref, a_ref, wo_ref, g_ref, wu_ref, wd_ref, o_ref, h_scr):
    @pl.when(pl.program_id(1) == 0)
    def _():
        x = x_ref[...] + jnp.dot(a_ref[...], wo_ref[...], preferred_element_type=F32)
        h_scr[...] = _rms(x, g_ref[...]).astype(BF16)
        o_ref[...] = x

    _mlp_chunk(h_scr, wu_ref, wd_ref, o_ref)


def _pool_ffn_kernel(x_ref, prev_ref, gm_ref, wp_ref, sc_ref, g_ref, wu_ref, wd_ref, o_ref, np_ref,
                     h_scr, ext_scr, *, tm, tiles_per_seq):
    @pl.when(pl.program_id(1) == 0)
    def _():
        ti = pl.program_id(0) % tiles_per_seq
        gm = gm_ref[...]
        x = x_ref[...]
        h = _rms(x, gm)
        hp = _rms(prev_ref[...], gm)
        hp = jnp.where(ti == 0, 0.0, hp)
        ext_scr[0:POOL_HALO, :] = hp
        ext_scr[POOL_HALO:POOL_HALO + tm, :] = h
        pos1 = ti * tm + lax.broadcasted_iota(jnp.int32, (tm, 1), 0) + 1
        gw = x.shape[1] // len(POOL_WINDOWS)
        for gi, w in enumerate(POOL_WINDOWS):
            lo = gi * gw
            hg = h[:, lo:lo + gw]
            s = hg
            for j in range(1, w):
                s = s + ext_scr[POOL_HALO - j:POOL_HALO - j + tm, lo:lo + gw]
            cnt = jnp.minimum(pos1, w).astype(F32)
            dlt = (s / cnt - hg).astype(BF16)
            y = jnp.dot(dlt, wp_ref[gi], preferred_element_type=F32)
            o_ref[:, lo:lo + gw] = x[:, lo:lo + gw] + y * sc_ref[:, lo:lo + gw]
        h_scr[...] = _rms(o_ref[...], g_ref[...]).astype(BF16)

        @pl.when(ti == tiles_per_seq - 1)
        def _():
            np_ref[...] = ext_scr[tm:tm + POOL_HALO, :]

    _mlp_chunk(h_scr, wu_ref, wd_ref, o_ref)


def _ffn(x2d, g, w_up, w_down, layer, *, tm, tf, proj=None, pool=None):
    n, d = x2d.shape
    dff = w_up.shape[-1]
    row = pl.BlockSpec((tm, d), lambda i, c: (i, 0))
    mlp_specs = [
        pl.BlockSpec((None, 1, d), lambda i, c: (layer, 0, 0)),
        pl.BlockSpec((None, d, tf), lambda i, c: (layer, 0, c)),
        pl.BlockSpec((None, tf, d), lambda i, c: (layer, c, 0)),
    ]
    mlp_args = [g, w_up, w_down]
    out_specs, out_shape = row, jax.ShapeDtypeStruct((n, d), F32)
    scratch = [pltpu.VMEM((tm, d), BF16)]
    if proj is not None:
        a2d, w_o, j = proj
        body, name = _proj_ffn_kernel, "proj_ffn"
        in_specs = [row, row, pl.BlockSpec((None, d, d), lambda i, c: (j, 0, 0))] + mlp_specs
        args = [x2d, a2d, w_o] + mlp_args
    elif pool is not None:
        gm, w_pool, scale, j, seq_len = pool
        tiles_per_seq = seq_len // tm
        halo_blocks = tm // POOL_HALO
        n_groups = len(POOL_WINDOWS)
        gw = d // n_groups
        body = functools.partial(_pool_ffn_kernel, tm=tm, tiles_per_seq=tiles_per_seq)
        name = "pool_ffn"
        in_specs = [
            row,
            pl.BlockSpec((POOL_HALO, d), lambda i, c: (jnp.maximum(i * halo_blocks - 1, 0), 0)),
            pl.BlockSpec((None, 1, d), lambda i, c: (layer, 0, 0)),
            pl.BlockSpec((None, n_groups, gw, gw), lambda i, c: (j, 0, 0, 0)),
            pl.BlockSpec((None, 1, d), lambda i, c: (j, 0, 0)),
        ] + mlp_specs
        args = [x2d, x2d, gm, w_pool, scale] + mlp_args
        out_specs = [row, pl.BlockSpec((POOL_HALO, d), lambda i, c: (i // tiles_per_seq, 0))]
        out_shape = [out_shape,
                     jax.ShapeDtypeStruct((n // seq_len * POOL_HALO, d), F32)]
        scratch = scratch + [pltpu.VMEM((POOL_HALO + tm, d), F32)]
    else:
        body, name = _ffn_kernel, "ffn"
        in_specs = [row] + mlp_specs
        args = [x2d] + mlp_args
    return pl.pallas_call(
        body,
        grid=(n // tm, dff // tf),
        in_specs=in_specs,
        out_specs=out_specs,
        out_shape=out_shape,
        scratch_shapes=scratch,
        compiler_params=_cparams("parallel", "arbitrary"),
        name=name,
    )(*args)


def _pool_sample_kernel(x_ref, st_ref, g_ref, wp_ref, sc_ref, o_ref, np_ref, *, pos0):
    ts = x_ref.shape[0]
    g = g_ref[...]
    xs = [x_ref[t] for t in range(ts)]
    ext = [st_ref[r] for r in range(POOL_BUF)] + [_rms(xt, g) for xt in xs]
    gw = xs[0].shape[1] // len(POOL_WINDOWS)
    for t in range(ts):
        e = POOL_BUF + t
        for gi, w in enumerate(POOL_WINDOWS):
            lo = gi * gw
            s = ext[e][:, lo:lo + gw]
            for jj in range(1, w):
                s = s + ext[e - jj][:, lo:lo + gw]
            cnt = float(min(pos0 + t + 1, w))
            dlt = (s / cnt - ext[e][:, lo:lo + gw]).astype(BF16)
            y = jnp.dot(dlt, wp_ref[gi], preferred_element_type=F32)
            o_ref[t, :, lo:lo + gw] = xs[t][:, lo:lo + gw] + y * sc_ref[:, lo:lo + gw]
    for r in range(POOL_BUF):
        np_ref[r] = ext[ts + r]


def _pool_sample(x_tm, state_tm, g_all, w_pool, scale_all, layer, j, *, pos0, bb):
    ts, b, d = x_tm.shape
    n_groups = len(POOL_WINDOWS)
    gw = d // n_groups
    return pl.pallas_call(
        functools.partial(_pool_sample_kernel, pos0=pos0),
        grid=(b // bb,),
        in_specs=[
            pl.BlockSpec((ts, bb, d), lambda i: (0, i, 0)),
            pl.BlockSpec((None, POOL_BUF, bb, d), lambda i: (j, 0, i, 0)),
            pl.BlockSpec((None, 1, d), lambda i: (layer, 0, 0)),
            pl.BlockSpec((None, n_groups, gw, gw), lambda i: (j, 0, 0, 0)),
            pl.BlockSpec((None, 1, d), lambda i: (j, 0, 0)),
        ],
        out_specs=[
            pl.BlockSpec((ts, bb, d), lambda i: (0, i, 0)),
            pl.BlockSpec((POOL_BUF, bb, d), lambda i: (0, i, 0)),
        ],
        out_shape=[
            jax.ShapeDtypeStruct((ts, b, d), F32),
            jax.ShapeDtypeStruct((POOL_BUF, b, d), F32),
        ],
        compiler_params=_cparams("parallel"),
        name="pool_sample",
    )(x_tm, state_tm, g_all, w_pool, scale_all)


def _qkv_kernel(x_ref, g_ref, wqk_ref, wv_ref, gq_ref, gk_ref, cos_ref, sin_ref, *rest, q_scale,
                write_slot):
    q_ref, kT32_ref, kT16_ref, v32_ref, v16_ref = rest[-5:]
    for w in range(kT32_ref.shape[0]):
        if w != write_slot:
            kT32_ref[w] = jnp.zeros(kT32_ref.shape[1:], F32)
            v32_ref[w] = jnp.zeros(v32_ref.shape[1:], F32)
    x = x_ref[0]
    tm, d = x.shape
    h = _rms(x, g_ref[...]).astype(BF16)
    v = jnp.dot(h, wv_ref[...], preferred_element_type=F32)
    v32_ref[write_slot, 0] = v
    v16_ref[0] = v.astype(BF16)
    cos = cos_ref[...][None]
    sin = sin_ref[...][None]
    rows = 4 * HEAD_DIM
    for part in range(2):
        gn = (gq_ref if part == 0 else gk_ref)[...][None]
        for c in range(d // rows):
            r0 = part * d + c * rows
            yT = lax.dot_general(wqk_ref[r0:r0 + rows, :], h, (((1,), (1,)), ((), ())),
                                 preferred_element_type=F32)
            y3 = yT.reshape(rows // HEAD_DIM, HEAD_DIM, tm)
            ms = jnp.mean(y3 * y3, axis=1, keepdims=True)
            yn = y3 * lax.rsqrt(ms + EPS) * gn
            x1 = yn[:, :HALF, :]
            x2 = yn[:, HALF:, :]
            out = jnp.concatenate([x1 * cos - x2 * sin, x2 * cos + x1 * sin], axis=1)
            out = out.reshape(rows, tm)
            if part == 0:
                q_ref[0, :, c * rows:(c + 1) * rows] = (out * q_scale).T.astype(BF16)
            else:
                kT32_ref[write_slot, 0, c * rows:(c + 1) * rows, :] = out
                kT16_ref[0, c * rows:(c + 1) * rows, :] = out.astype(BF16)


def _qkv(x, g_all, wqkT, wv, gq, gk, cosT, sinT, layer, j, tab_map, *, tm, n_slots=1, slot=0,
         prev=None):
    b, t, d = x.shape
    tok = lambda bi, i: (bi, i, 0)
    tr = lambda bi, i: (bi, 0, i)
    in_specs = [
        pl.BlockSpec((1, tm, d), tok),
        pl.BlockSpec((None, 1, d), lambda bi, i: (layer, 0, 0)),
        pl.BlockSpec((None, 2 * d, d), lambda bi, i: (j, 0, 0)),
        pl.BlockSpec((None, d, d), lambda bi, i: (j, 0, 0)),
        pl.BlockSpec((None, HEAD_DIM, 1), lambda bi, i: (j, 0, 0)),
        pl.BlockSpec((None, HEAD_DIM, 1), lambda bi, i: (j, 0, 0)),
        pl.BlockSpec((HALF, tm), tab_map),
        pl.BlockSpec((HALF, tm), tab_map),
    ]
    args = [x, g_all, wqkT, wv, gq, gk, cosT, sinT]
    aliases = {}
    n_blk, blk, write_slot = n_slots, 0, slot
    if prev is not None:
        aliases = {len(args): 1, len(args) + 1: 3}
        in_specs = in_specs + [pl.BlockSpec(memory_space=pl.ANY)] * 2
        args = args + list(prev)
        n_blk, blk, write_slot = 1, slot, 0
    return pl.pallas_call(
        functools.partial(_qkv_kernel, q_scale=Q_SCALE, write_slot=write_slot),
        grid=(b, t // tm),
        in_specs=in_specs,
        out_specs=[
            pl.BlockSpec((1, tm, d), tok),
            pl.BlockSpec((n_blk, 1, d, tm), lambda bi, i: (blk, bi, 0, i)),
            pl.BlockSpec((1, d, tm), tr),
            pl.BlockSpec((n_blk, 1, tm, d), lambda bi, i: (blk, bi, i, 0)),
            pl.BlockSpec((1, tm, d), tok),
        ],
        out_shape=[
            jax.ShapeDtypeStruct((b, t, d), BF16),
            jax.ShapeDtypeStruct((n_slots, b, d, t), F32),
            jax.ShapeDtypeStruct((b, d, t), BF16),
            jax.ShapeDtypeStruct((n_slots, b, t, d), F32),
            jax.ShapeDtypeStruct((b, t, d), BF16),
        ],
        input_output_aliases=aliases,
        compiler_params=_cparams("parallel", "parallel"),
        name="qkv",
    )(*args)


def _lambda_full(lam_ref, lam_init):
    lp = lam_ref[...]
    a = jnp.sum(lp[0:1] * lp[1:2], axis=-1, keepdims=True)
    b = jnp.sum(lp[2:3] * lp[3:4], axis=-1, keepdims=True)
    return jnp.exp(a) - jnp.exp(b) + lam_init


def _diff_combine(o1, l1, o2, l2, lam, subln, lam_init):
    o = o1 / l1 - lam * (o2 / l2)
    return _rms(o, subln, SUBLN_EPS) * (1.0 - lam_init)


def _sample_attention(i, q_ref, kn_ref, vn_ref, k_page, v_page, n_pages, o_ref, s_scr, p_scr, lam,
                      subln, lam_init, bounded):
    q = q_ref[i].astype(F32)
    kn = kn_ref[i]
    vn = vn_ref[i]
    ts = q.shape[0]
    vd = 2 * HEAD_DIM
    n_heads = q.shape[1] // vd
    n_hc = 2 * n_heads
    causal = (lax.broadcasted_iota(jnp.int32, (ts, ts), 1)
              <= lax.broadcasted_iota(jnp.int32, (ts, ts), 0))
    nt = (((1,), (1,)), ((), ()))
    qh = [q[:, hc * HEAD_DIM:(hc + 1) * HEAD_DIM] for hc in range(n_hc)]
    for hc in range(n_hc):
        for p in range(0, n_pages, 2):
            k2 = jnp.concatenate([k_page(p, hc), k_page(p + 1, hc)], axis=1)
            s_scr[hc, :, p * PAGE_SIZE:(p + 2) * PAGE_SIZE] = jnp.dot(
                qh[hc], k2, preferred_element_type=F32)
    p_new, row_sum = [], []
    for hc in range(n_hc):
        s = s_scr[hc]
        s_new = lax.dot_general(qh[hc], kn[:, hc * HEAD_DIM:(hc + 1) * HEAD_DIM], nt,
                                preferred_element_type=F32)
        s_new = jnp.where(causal, s_new, -jnp.inf)
        if bounded:
            pr = jnp.exp2(s)
            pn = jnp.exp2(s_new)
        else:
            m = jnp.maximum(jnp.max(s, axis=1, keepdims=True),
                            jnp.max(s_new, axis=1, keepdims=True))
            pr = jnp.exp2(s - m)
            pn = jnp.exp2(s_new - m)
        p_scr[hc // 4, (hc % 4) * ts:(hc % 4 + 1) * ts, :] = pr
        p_new.append(pn)
        row_sum.append(jnp.sum(pr, axis=1, keepdims=True) + jnp.sum(pn, axis=1, keepdims=True))
    for h in range(0, n_heads, 2):
        pn4 = jnp.concatenate(p_new[2 * h:2 * h + 4], axis=0)
        acc = jnp.dot(pn4, vn[:, h * vd:(h + 2) * vd], preferred_element_type=F32)
        for p in range(n_pages):
            v2 = jnp.concatenate([v_page(p, h), v_page(p, h + 1)], axis=1)
            acc = acc + jnp.dot(p_scr[h // 2, :, p * PAGE_SIZE:(p + 1) * PAGE_SIZE], v2,
                                preferred_element_type=F32)
        for a in range(2):
            blk = acc[2 * a * ts:2 * (a + 1) * ts, a * vd:(a + 1) * vd]
            o = _diff_combine(blk[:ts], row_sum[2 * (h + a)], blk[ts:], row_sum[2 * (h + a) + 1],
                              lam, subln, lam_init)
            o_ref[i, :, (h + a) * vd:(h + a + 1) * vd] = o.astype(BF16)


def _attn_kernel(bounded_ref, pt_ref, qp_ref, kT_ref, v_ref, lam_ref, subln_ref, qs_ref, kn_ref,
                 vn_ref, ck_hbm, cv_hbm, o_ref, os_ref, qs_scr, m_scr, acc_scr, s_scr, p_scr, kbuf,
                 vbuf, sem, *, layer, n_seq, n_pages, tq, tk, rc, lam_init):
    qi = pl.program_id(2)
    vd = v_ref.shape[2]
    n_heads = pl.num_programs(1)
    step = (pl.program_id(0) * n_heads + pl.program_id(1)) * pl.num_programs(2) + qi
    n_steps = pl.num_programs(0) * n_heads * pl.num_programs(2)
    slot = step % 2

    def page_copies(for_step, to_slot, lookup=True):
        copies = []
        for idx in range(n_seq * n_pages):
            page = pt_ref[for_step * (n_seq * n_pages) + idx] if lookup else 0
            copies.append(pltpu.make_async_copy(ck_hbm.at[layer, page], kbuf.at[to_slot, idx],
                                                sem.at[to_slot, 0]))
            copies.append(pltpu.make_async_copy(cv_hbm.at[layer, page], vbuf.at[to_slot, idx],
                                                sem.at[to_slot, 1]))
        return copies

    @pl.when(step == 0)
    def _():
        for c in page_copies(0, 0):
            c.start()

    @pl.when(step + 1 < n_steps)
    def _():
        for c in page_copies(step + 1, 1 - slot):
            c.start()

    for c in page_copies(step, slot, lookup=False):
        c.wait()

    lam = _lambda_full(lam_ref, lam_init)
    subln = subln_ref[...]

    q = qp_ref[0]
    lane = lax.broadcasted_iota(jnp.int32, q.shape, 1)
    qs_scr[0:tq, :] = jnp.where(lane < HEAD_DIM, q, jnp.zeros_like(q))
    qs_scr[tq:2 * tq, :] = jnp.where(lane >= HEAD_DIM, q, jnp.zeros_like(q))
    acc_scr[...] = jnp.zeros(acc_scr.shape, F32)
    ones = jnp.ones((max(tk, tq), vd), BF16)
    lower_tri = (lax.broadcasted_iota(jnp.int32, (rc, rc), 1)
                 <= lax.broadcasted_iota(jnp.int32, (rc, rc), 0))

    def keys(k0, n):
        kT = kT_ref[0, :, pl.ds(k0, n)]
        vv = jnp.concatenate([v_ref[0, pl.ds(k0, n), :], ones[:n]], axis=1)
        return kT, vv

    def update(r, kT, vv, bounded, triangle=False):
        s = jnp.dot(qs_scr[r:r + rc, :], kT, preferred_element_type=F32)
        if triangle:
            n = s.shape[1]
            tri = jnp.where(lower_tri, s[:, n - rc:], -jnp.inf)
            s = tri if n == rc else jnp.concatenate([s[:, :n - rc], tri], axis=1)
        if bounded:
            pv = jnp.dot(jnp.exp2(s).astype(BF16), vv, preferred_element_type=F32)
            acc_scr[r:r + rc, :] += pv
        else:
            m_old = m_scr[r:r + rc, :]
            m_new = jnp.maximum(m_old, jnp.max(s, axis=1, keepdims=True))
            alpha = jnp.exp2(m_old - m_new)
            p = jnp.exp2(s - m_new)
            pv = jnp.dot(p.astype(BF16), vv, preferred_element_type=F32)
            acc_scr[r:r + rc, :] = acc_scr[r:r + rc, :] * alpha + pv
            m_scr[r:r + rc, :] = m_new

    def run(bounded):
        def block(jb):
            kT, vv = keys(pl.multiple_of(jb * tk, tk), tk)
            for r in range(0, 2 * tq, rc):
                update(r, kT, vv, bounded)

        def body(jp, carry):
            block(2 * jp)
            block(2 * jp + 1)
            return carry

        n_below = qi * (tq // tk)
        lax.fori_loop(0, n_below // 2, body, 0)

        @pl.when(n_below % 2 == 1)
        def _():
            block(n_below - 1)

        for i in range(n_seq):
            k_page = lambda p, hc, i=i: kbuf[slot, i * n_pages + p, hc]
            v_page = lambda p, h, i=i: vbuf.at[slot, i * n_pages + p][
                pl.ds(h, PAGE_SIZE, stride=vbuf.shape[2] // PAGE_SIZE), :]
            _sample_attention(i, qs_ref, kn_ref, vn_ref, k_page, v_page, n_pages, os_ref, s_scr,
                              p_scr, lam, subln, lam_init, bounded)
        base = pl.multiple_of(qi * tq, tq)
        for r in range(0, 2 * tq, rc):
            kT, vv = keys(base, (r % tq) + rc)
            update(r, kT, vv, bounded, triangle=True)
        acc = acc_scr[...]
        o = _diff_combine(acc[0:tq, 0:vd], acc[0:tq, vd:vd + 1], acc[tq:, 0:vd],
                          acc[tq:, vd:vd + 1], lam, subln, lam_init)
        o_ref[0] = o.astype(BF16)

    is_bounded = bounded_ref[0] == 1

    @pl.when(is_bounded)
    def _():
        run(True)

    @pl.when(jnp.logical_not(is_bounded))
    def _():
        m_scr[...] = jnp.full(m_scr.shape, -jnp.inf, F32)
        run(False)


def _attention(bounded, page_table_flat, q_p, kT16, v16, q_s, kn_s, vn_s, lam_params, subln_all,
               cache_kT, cache_v, j, lam_init, *, tq, tk, rc):
    b, t, d = q_p.shape
    bs, ts, _ = q_s.shape
    vd = 2 * HEAD_DIM
    n_heads = d // vd
    nq = t // tq
    n_steps = b * n_heads * nq
    n_seq = bs // n_steps
    assert n_seq * n_steps == bs, "sample sequences must split evenly over the prompt grid steps"
    n_pages = page_table_flat.shape[0] // bs
    n_hc = cache_kT.shape[2]

    def step(bi, hi, qi):
        return (bi * n_heads + hi) * nq + qi

    seq_rows = pl.BlockSpec((n_seq, ts, d), lambda bi, hi, qi, fl, pt: (step(bi, hi, qi), 0, 0))
    sample_specs = [seq_rows] * 3 + [pl.BlockSpec(memory_space=pl.ANY)] * 2
    sample_args = [q_s, kn_s, vn_s, cache_kT, cache_v]
    grid_spec = pltpu.PrefetchScalarGridSpec(
        num_scalar_prefetch=2,
        grid=(b, n_heads, nq),
        in_specs=[
            pl.BlockSpec((1, tq, vd), lambda bi, hi, qi, fl, pt: (bi, qi, hi)),
            pl.BlockSpec((1, vd, t), lambda bi, hi, qi, fl, pt: (bi, hi, 0)),
            pl.BlockSpec((1, t, vd), lambda bi, hi, qi, fl, pt: (bi, 0, hi)),
            pl.BlockSpec((None, 4, HEAD_DIM), lambda bi, hi, qi, fl, pt: (j, 0, 0)),
            pl.BlockSpec((None, 1, vd), lambda bi, hi, qi, fl, pt: (j, 0, 0)),
        ] + sample_specs,
        out_specs=[pl.BlockSpec((1, tq, vd), lambda bi, hi, qi, fl, pt: (bi, qi, hi)), seq_rows],
        scratch_shapes=[
            pltpu.VMEM((2 * tq, vd), BF16),
            pltpu.VMEM((2 * tq, 1), F32),
            pltpu.VMEM((2 * tq, 2 * vd), F32),
            pltpu.VMEM((n_hc, ts, n_pages * PAGE_SIZE), F32),
            pltpu.VMEM((n_hc // 4, 4 * ts, n_pages * PAGE_SIZE), F32),
            pltpu.VMEM((2, n_seq * n_pages) + cache_kT.shape[2:], F32),
            pltpu.VMEM((2, n_seq * n_pages) + cache_v.shape[2:], F32),
            pltpu.SemaphoreType.DMA((2, 2)),
        ],
    )
    return pl.pallas_call(
        functools.partial(_attn_kernel, layer=j, n_seq=n_seq, n_pages=n_pages, tq=tq, tk=tk, rc=rc,
                          lam_init=lam_init),
        grid_spec=grid_spec,
        out_shape=[jax.ShapeDtypeStruct((b, t, d), BF16), jax.ShapeDtypeStruct((bs, ts, d), BF16)],
        compiler_params=_cparams("arbitrary", "arbitrary", "arbitrary"),
        name="attention",
    )(bounded, page_table_flat, q_p, kT16, v16, lam_params, subln_all, *sample_args)


def _scores_bounded(q_gain, k_gain):
    bound = (math.sqrt(HEAD_DIM) * math.log2(math.e)
             * jnp.max(jnp.abs(q_gain), axis=-1) * jnp.max(jnp.abs(k_gain), axis=-1))
    return (bound <= SCORE_BOUND_LOG2).astype(jnp.int32)


def _rope_tables(pos):
    inv = ROPE_THETA ** (-jnp.arange(HALF, dtype=F32) / HALF)
    ang = inv[:, None] * pos.astype(F32)[None, :]
    return jnp.cos(ang), jnp.sin(ang)


def kernel(x_prompt, x_sample, cache_k, cache_v, page_table, state_pool, norm_mix, norm_ffn, w_pool,
           pool_scale, w_qkv, q_norm, k_norm, lambda_q1, lambda_k1, lambda_q2, lambda_k2, subln, w_o,
           w_up, w_down):
    bp, sp, d = x_prompt.shape
    bs, ts, _ = x_sample.shape
    depth = norm_mix.shape[0]
    past_len = page_table.shape[1] * PAGE_SIZE
    n_hc = d // HEAD_DIM
    n_heads = n_hc // 2
    vd = 2 * HEAD_DIM

    w_up16 = w_up.astype(BF16)
    w_down16 = w_down.astype(BF16)
    w_pool16 = w_pool.astype(BF16)
    w_o16 = w_o.astype(BF16)
    wqkT16 = jnp.swapaxes(w_qkv[:, :, :2 * d], 1, 2).astype(BF16)
    wv16 = w_qkv[:, :, 2 * d:].astype(BF16)
    gq = q_norm[:, :, None]
    gk = k_norm[:, :, None]
    lam_params = jnp.stack([lambda_q1, lambda_k1, lambda_q2, lambda_k2], axis=1)
    norm_mix = norm_mix[:, None, :]
    norm_ffn = norm_ffn[:, None, :]
    pool_scale = pool_scale[:, None, :]
    subln = subln[:, None, :]

    cos_p, sin_p = _rope_tables(jnp.arange(sp))
    cos_s, sin_s = _rope_tables(past_len + jnp.tile(jnp.arange(ts), bs))

    cache_kT = jnp.transpose(cache_k, (0, 1, 3, 4, 2))
    state_tm = jnp.transpose(state_pool, (0, 2, 1, 3))
    cache_v2 = cache_v.reshape(cache_v.shape[0], cache_v.shape[1], PAGE_SIZE * n_heads, vd)
    pt_flat = page_table.reshape(-1)
    bounded = _scores_bounded(q_norm, k_norm)

    n_attn = depth // N_MIXERS
    mlp = functools.partial(_ffn, tm=FFN_ROWS, tf=FFN_COLS)
    xp = x_prompt.reshape(bp * sp, d)
    xs = x_sample.reshape(bs * ts, d)
    kv_p = None
    nk_s, nv_s, np_p, np_s = [], [], [], []
    for i in range(depth):
        j = i // N_MIXERS
        if i % N_MIXERS == 0:
            xp, pool_p = mlp(xp, norm_ffn, w_up16, w_down16, i,
                             pool=(norm_mix, w_pool16, pool_scale, j, sp))
            np_p.append(pool_p.reshape(bp, POOL_HALO, d)[:, POOL_HALO - POOL_BUF:])
            xs_tm = jnp.swapaxes(xs.reshape(bs, ts, d), 0, 1)
            xs_tm, pool_s = _pool_sample(xs_tm, state_tm, norm_mix, w_pool16, pool_scale, i, j,
                                         pos0=past_len, bb=32)
            xs = jnp.swapaxes(xs_tm, 0, 1).reshape(bs * ts, d)
            np_s.append(pool_s)
            xs = mlp(xs, norm_ffn, w_up16, w_down16, i)
        else:
            lam_init = 0.8 - 0.6 * math.exp(-0.3 * i)
            q_p, kT32, kT16, v32, v16 = _qkv(xp.reshape(bp, sp, d), norm_mix, wqkT16, wv16, gq, gk,
                                             cos_p, sin_p, i, j, lambda bi, ti: (0, ti), tm=512,
                                             n_slots=n_attn, slot=j, prev=kv_p)
            kv_p = (kT32, v32)
            q_s, kT32_s, _, v32_s, _ = _qkv(xs[None], norm_mix, wqkT16, wv16, gq, gk, cos_s, sin_s,
                                            i, j, lambda bi, ti: (0, ti), tm=512)
            q_s = q_s.reshape(bs, ts, d)
            kn_s = kT32_s[0, 0].T.reshape(bs, ts, d)
            vn_s = v32_s.reshape(bs, ts, d)
            o_p, o_s = _attention(bounded[j:j + 1], pt_flat, q_p, kT16, v16, q_s, kn_s, vn_s,
                                  lam_params, subln, cache_kT, cache_v2, j, lam_init,
                                  tq=1024, tk=1024, rc=256)
            xp = mlp(xp, norm_ffn, w_up16, w_down16, i, proj=(o_p.reshape(bp * sp, d), w_o16, j))
            xs = mlp(xs, norm_ffn, w_up16, w_down16, i, proj=(o_s.reshape(bs * ts, d), w_o16, j))
            nk_s.append(kn_s.reshape(bs, ts, n_hc, HEAD_DIM))
            nv_s.append(vn_s.reshape(bs, ts, n_heads, vd))

    kT32, v32 = kv_p
    new_k_prompt = jnp.transpose(kT32.reshape(n_attn, bp, n_hc, HEAD_DIM, sp), (0, 1, 4, 2, 3))
    new_v_prompt = v32.reshape(n_attn, bp, sp, n_heads, vd)
    new_k_sample = jnp.stack(nk_s)
    new_v_sample = jnp.stack(nv_s)
    new_pool_prompt = jnp.stack(np_p)
    new_pool_sample = jnp.transpose(jnp.stack(np_s), (0, 2, 1, 3))
    return (xp.reshape(bp, sp, d), xs.reshape(bs, ts, d), new_k_prompt, new_v_prompt, new_k_sample,
            new_v_sample, new_pool_prompt, new_pool_sample)
```
